```python
import jax, jax.numpy as jnp
from jax import lax
import numpy as np

D_MODEL = 1024
BATCH = 32
SEQ = 2048
DEPTH = 2
DEC_BATCH = 16
DEC_SEQ = 4096
PAST_LEN = 128

N_META = 16
N_MIXERS = 2
N_RWKV = (DEPTH + 1) // 2
N_FNET = DEPTH // 2
HEAD_SIZE = 64
N_HEADS = D_MODEL // HEAD_SIZE
DECAY_LORA = 64
AAA_LORA = 64
GATE_LORA = 160
N_MU = 6
FNET_GROUPS = 8
FNET_GROUP_DIM = D_MODEL // FNET_GROUPS
D_FF = 2816
CONV_WIDTH = 3
RMS_EPS = 1e-6
GN_EPS = 64e-5

kernel_name = 'bidir_rwkv7_fnet_convglu_trunk'


def rmsnorm(x, g):
    xf = x.astype(jnp.float32)
    y = xf * lax.rsqrt(jnp.mean(xf * xf, axis=-1, keepdims=True) + RMS_EPS)
    return (y * g.astype(jnp.float32)).astype(x.dtype)


def split_heads(z):
    return z.reshape(z.shape[:-1] + (N_HEADS, HEAD_SIZE))


def _wkv_step(S, inp):
    r, decay, k, v, avec, b = inp
    sa = jnp.einsum('bhij,bhj->bhi', S, avec)
    S = S * decay[..., None, :] + sa[..., :, None] * b[..., None, :] + v[..., :, None] * k[..., None, :]
    out = jnp.einsum('bhij,bhj->bhi', S, r)
    return S, out


def wkv_scan(r, decay, k, v, avec, b, reverse):
    B, T, H, N = r.shape
    xs = tuple(jnp.moveaxis(z, 1, 0) for z in (r, decay, k, v, avec, b))
    S0 = jnp.zeros((B, H, N, N), jnp.float32)
    _, out = lax.scan(_wkv_step, S0, xs, reverse=reverse)
    return jnp.moveaxis(out, 0, 1)


def rwkv7_time_mix(h, mu, w_rkv, w0, w1, w2, a0, a1, a2, g1, g2, k_k, k_a, r_k, gn_w, gn_b, w_o):
    B, T, D = h.shape
    f32 = jnp.float32
    hp = jnp.pad(h, ((0, 0), (1, 1), (0, 0)))
    xx = 0.5 * (hp[:, :-2] + hp[:, 2:]) - h
    xr, xw, xk, xv, xa, xg = [h + xx * mu[i] for i in range(N_MU)]
    r = xr @ w_rkv[0]
    k = xk @ w_rkv[1]
    v = xv @ w_rkv[2]
    g = jax.nn.sigmoid(xg @ g1) @ g2
    w_lin = w0[:, None, None, :] + jnp.einsum('zbtr,zrd->zbtd', jnp.tanh(jnp.einsum('btd,zdr->zbtr', xw, w1)), w2)
    w_log = -jax.nn.softplus(-w_lin.astype(f32)) - 0.5
    decay = jnp.exp(-jnp.exp(w_log))
    a = jax.nn.sigmoid((a0[:, None, None, :] + jnp.einsum('zbtr,zrd->zbtd', jnp.einsum('btd,zdr->zbtr', xa, a1), a2)).astype(f32))
    rf = split_heads(r.astype(f32))
    vf = split_heads(v.astype(f32))
    kf = k.astype(f32)
    kk = split_heads(kf * k_k.astype(f32))
    kk = kk / jnp.maximum(jnp.sqrt(jnp.sum(kk * kk, axis=-1, keepdims=True)), 1e-12)
    k_dir = split_heads(kf[None] * (1.0 + (a - 1.0) * k_a.astype(f32)))
    b_dir = kk[None] * split_heads(a)
    decay_h = split_heads(decay)
    o_fwd = wkv_scan(rf, decay_h[0], k_dir[0], vf, -kk, b_dir[0], reverse=False)
    o_bwd = wkv_scan(rf, decay_h[1], k_dir[1], vf, -kk, b_dir[1], reverse=True)
    o = o_fwd + o_bwd
    mean = jnp.mean(o, axis=-1, keepdims=True)
    var = jnp.mean(jnp.square(o - mean), axis=-1, keepdims=True)
    o = ((o - mean) * lax.rsqrt(var + GN_EPS)).reshape(B, T, D) * gn_w.astype(f32) + gn_b.astype(f32)
    bonus = jnp.sum(jnp.sum(rf[None] * k_dir * r_k.astype(f32), axis=-1, keepdims=True), axis=0)
    o = o + (bonus * vf).reshape(B, T, D)
    return (o.astype(h.dtype) * g) @ w_o


def fourier_mix(h, w_f):
    B, T, D = h.shape
    hg = h.astype(jnp.float32).reshape(B, T, FNET_GROUPS, FNET_GROUP_DIM)
    f = jnp.fft.fftn(hg, axes=(1, 3), norm='ortho').real
    return f.reshape(B, T, D).astype(h.dtype) @ w_f


def conv_glu_ffn(h, w_in, conv_w, conv_b, w_out):
    u = h @ w_in
    act_in, lin = u[..., :D_FF], u[..., D_FF:]
    p = jnp.pad(act_in, ((0, 0), (1, 1), (0, 0)))
    c = p[:, :-2] * conv_w[0] + p[:, 1:-1] * conv_w[1] + p[:, 2:] * conv_w[2] + conv_b
    return (jax.nn.silu(c) * lin) @ w_out


def trunk(x, meta_tokens, norm_mix, norm_ffn, norm_final,
          rwkv_mu, rwkv_w_rkv, rwkv_w0, rwkv_w1, rwkv_w2, rwkv_a0, rwkv_a1, rwkv_a2,
          rwkv_g1, rwkv_g2, rwkv_k_k, rwkv_k_a, rwkv_r_k, rwkv_gn_w, rwkv_gn_b, rwkv_w_o,
          fnet_w_o, ffn_w_in, ffn_conv_w, ffn_conv_b, ffn_w_out):
    B = x.shape[0]
    meta = jnp.broadcast_to(meta_tokens.astype(x.dtype)[None], (B, N_META, D_MODEL))
    h = jnp.concatenate([meta, x], axis=1)
    for i in range(DEPTH):
        hn = rmsnorm(h, norm_mix[i])
        j = i // N_MIXERS
        if i % N_MIXERS == 0:
            h = h + rwkv7_time_mix(hn, rwkv_mu[j], rwkv_w_rkv[j], rwkv_w0[j], rwkv_w1[j], rwkv_w2[j],
                                   rwkv_a0[j], rwkv_a1[j], rwkv_a2[j], rwkv_g1[j], rwkv_g2[j],
                                   rwkv_k_k[j], rwkv_k_a[j], rwkv_r_k[j], rwkv_gn_w[j], rwkv_gn_b[j], rwkv_w_o[j])
        else:
            h = h + fourier_mix(hn, fnet_w_o[j])
        h = h + conv_glu_ffn(rmsnorm(h, norm_ffn[i]), ffn_w_in[i], ffn_conv_w[i], ffn_conv_b[i], ffn_w_out[i])
    return rmsnorm(h, norm_final)[:, N_META:]


def setup_inputs(seed: int = 0) -> dict:
    key = jax.random.key(seed)
    ks = jax.random.split(key, 32)
    nrm = jax.random.normal
    D = D_MODEL
    return {
        'x_prompt': nrm(ks[0], (BATCH, SEQ, D), jnp.float32),
        'x_sample': nrm(ks[1], (DEC_BATCH, DEC_SEQ, D), jnp.float32),
        'meta_tokens': nrm(ks[2], (N_META, D), jnp.float32),
        'norm_mix': 1.0 + 0.02 * nrm(ks[3], (DEPTH, D), jnp.float32),
        'norm_ffn': 1.0 + 0.02 * nrm(ks[4], (DEPTH, D), jnp.float32),
        'norm_final': 1.0 + 0.02 * nrm(ks[5], (D,), jnp.float32),
        'rwkv_mu': jax.random.uniform(ks[6], (N_RWKV, N_MU, D), jnp.float32),
        'rwkv_w_rkv': nrm(ks[7], (N_RWKV, 3, D, D), jnp.float32) * D ** -0.5,
        'rwkv_w0': jax.random.uniform(ks[8], (N_RWKV, 2, D), jnp.float32, -5.0, 1.0),
        'rwkv_w1': nrm(ks[9], (N_RWKV, 2, D, DECAY_LORA), jnp.float32) * D ** -0.5,
        'rwkv_w2': nrm(ks[10], (N_RWKV, 2, DECAY_LORA, D), jnp.float32) * 0.1 * DECAY_LORA ** -0.5,
        'rwkv_a0': 0.5 * nrm(ks[11], (N_RWKV, 2, D), jnp.float32),
        'rwkv_a1': nrm(ks[12], (N_RWKV, 2, D, AAA_LORA), jnp.float32) * D ** -0.5,
        'rwkv_a2': nrm(ks[13], (N_RWKV, 2, AAA_LORA, D), jnp.float32) * 0.1 * AAA_LORA ** -0.5,
        'rwkv_g1': nrm(ks[14], (N_RWKV, D, GATE_LORA), jnp.float32) * D ** -0.5,
        'rwkv_g2': nrm(ks[15], (N_RWKV, GATE_LORA, D), jnp.float32) * GATE_LORA ** -0.5,
        'rwkv_k_k': 0.85 + 0.02 * nrm(ks[16], (N_RWKV, D), jnp.float32),
        'rwkv_k_a': 1.0 + 0.02 * nrm(ks[17], (N_RWKV, D), jnp.float32),
        'rwkv_r_k': 0.1 * nrm(ks[18], (N_RWKV, N_HEADS, HEAD_SIZE), jnp.float32),
        'rwkv_gn_w': 1.0 + 0.02 * nrm(ks[19], (N_RWKV, D), jnp.float32),
        'rwkv_gn_b': 0.02 * nrm(ks[20], (N_RWKV, D), jnp.float32),
        'rwkv_w_o': nrm(ks[21], (N_RWKV, D, D), jnp.float32) * D ** -0.5,
        'fnet_w_o': nrm(ks[22], (N_FNET, D, D), jnp.float32) * D ** -0.5,
        'ffn_w_in': nrm(ks[23], (DEPTH, D, 2 * D_FF), jnp.float32) * D ** -0.5,
        'ffn_conv_w': nrm(ks[24], (DEPTH, CONV_WIDTH, D_FF), jnp.float32) * CONV_WIDTH ** -0.5,
        'ffn_conv_b': 0.02 * nrm(ks[25], (DEPTH, D_FF), jnp.float32),
        'ffn_w_out': nrm(ks[26], (DEPTH, D_FF, D), jnp.float32) * D_FF ** -0.5,
    }


def reference(x_prompt, x_sample, meta_tokens, norm_mix, norm_ffn, norm_final,
              rwkv_mu, rwkv_w_rkv, rwkv_w0, rwkv_w1, rwkv_w2, rwkv_a0, rwkv_a1, rwkv_a2,
              rwkv_g1, rwkv_g2, rwkv_k_k, rwkv_k_a, rwkv_r_k, rwkv_gn_w, rwkv_gn_b, rwkv_w_o,
              fnet_w_o, ffn_w_in, ffn_conv_w, ffn_conv_b, ffn_w_out):
    y_prompt = trunk(x_prompt, meta_tokens, norm_mix, norm_ffn, norm_final,
                     rwkv_mu, rwkv_w_rkv, rwkv_w0, rwkv_w1, rwkv_w2, rwkv_a0, rwkv_a1, rwkv_a2,
                     rwkv_g1, rwkv_g2, rwkv_k_k, rwkv_k_a, rwkv_r_k, rwkv_gn_w, rwkv_gn_b, rwkv_w_o,
                     fnet_w_o, ffn_w_in, ffn_conv_w, ffn_conv_b, ffn_w_out)
    y_sample = trunk(x_sample, meta_tokens, norm_mix, norm_ffn, norm_final,
                     rwkv_mu, rwkv_w_rkv, rwkv_w0, rwkv_w1, rwkv_w2, rwkv_a0, rwkv_a1, rwkv_a2,
                     rwkv_g1, rwkv_g2, rwkv_k_k, rwkv_k_a, rwkv_r_k, rwkv_gn_w, rwkv_gn_b, rwkv_w_o,
                     fnet_w_o, ffn_w_in, ffn_conv_w, ffn_conv_b, ffn_w_out)
    return (y_prompt, y_sample)
```

```python
import functools
import math

import jax
import jax.numpy as jnp
import numpy as np
from jax import lax
from jax.experimental import pallas as pl
from jax.experimental.pallas import tpu as pltpu

D_MODEL = 1024
N_META = 16
HEAD_SIZE = 64
N_HEADS = D_MODEL // HEAD_SIZE
GATE_LORA_PAD = 256
LORA2 = 128
D_FF = 2816
FNET_GROUPS = 8
FNET_GROUP_DIM = D_MODEL // FNET_GROUPS
RMS_EPS = 1e-6
GN_EPS = 64e-5

CHUNK = 64
PAIR = 2 * HEAD_SIZE
N_PAIRS = N_HEADS // 2
HALO = 8
FF_BLOCK = D_FF // 2
VMEM_LIMIT_BYTES = 56 * 1024 * 1024

BF16 = jnp.bfloat16
F32 = jnp.float32


def _dot(a, b):
    return jnp.dot(a, b, preferred_element_type=F32)


def _dot_nt(a, b):
    return lax.dot_general(a, b, (((1,), (1,)), ((), ())), preferred_element_type=F32)


def _dot_tn(a, b):
    return lax.dot_general(a, b, (((0,), (0,)), ((), ())), preferred_element_type=F32)


def _rmsnorm(x, gain):
    ms = jnp.mean(x * x, axis=-1, keepdims=True)
    return x * lax.rsqrt(ms + RMS_EPS) * gain


def _split_bf16(x):
    hi = x.astype(BF16)
    lo = (x - hi.astype(F32)).astype(BF16)
    return hi, lo


def _head_sum_bcast(x, e_ref, et_ref):
    s = _dot(x.astype(BF16), e_ref[...])
    hi, lo = _split_bf16(s)
    return _dot(hi, et_ref[...]) + _dot(lo, et_ref[...])


def _normed_window(h_ref, hp_ref, hn_ref, gain, i, nblk, tb, tr):
    row0 = i * tb
    rows = row0 + lax.broadcasted_iota(jnp.int32, (tb, 1), 0)
    valid = rows < tr
    x = jnp.where(valid, _rmsnorm(h_ref[...], gain), 0.0)
    xp = _rmsnorm(hp_ref[...], gain)[HALO - 1:HALO, :]
    xp = jnp.where(i > 0, xp, 0.0)
    xn = _rmsnorm(hn_ref[...], gain)[0:1, :]
    xn = jnp.where(jnp.logical_and(i < nblk - 1, row0 + tb < tr), xn, 0.0)
    return x, xp, xn, valid


def _shift_rows(x, xp, xn, tb):
    ridx = lax.broadcasted_iota(jnp.int32, (tb, 1), 0)
    x_prev = jnp.where(ridx == 0, xp, pltpu.roll(x, 1, 0))
    x_next = jnp.where(ridx == tb - 1, xn, pltpu.roll(x, tb - 1, 0))
    return x_prev, x_next


def _proj_kernel(h_ref, hp_ref, hn_ref, gain_ref, mu_ref, wrkv_ref, w1_ref, w2_ref, w0_ref,
                 a1_ref, a2_ref, a0_ref, g1_ref, g2_ref, kk_ref, e_ref, et_ref,
                 r_out, k_out, v_out, kk_out, g_out, a_out, lw_out, *, tb, tr, nblk):
    i = pl.program_id(1)
    x, xp, xn, valid = _normed_window(h_ref, hp_ref, hn_ref, gain_ref[...], i, nblk, tb, tr)
    x_prev, x_next = _shift_rows(x, xp, xn, tb)
    xx = jnp.where(valid, 0.5 * (x_prev + x_next) - x, 0.0)
    mu = mu_ref[...]

    def mix(j):
        return (x + xx * mu[j:j + 1, :]).astype(BF16)

    r = _dot(mix(0), wrkv_ref[0])
    k = _dot(mix(2), wrkv_ref[1])
    v = _dot(mix(3), wrkv_ref[2])
    r_out[...] = r.astype(BF16)
    k_out[...] = k.astype(BF16)
    v_out[...] = v.astype(BF16)

    tw = jnp.tanh(_dot(mix(1), w1_ref[...]))
    w_lin = w0_ref[...] + _dot(tw.astype(BF16), w2_ref[...])
    lw_out[...] = (-math.exp(-0.5)) * jax.nn.sigmoid(w_lin)

    ta = _dot(mix(4), a1_ref[...])
    a_out[...] = jax.nn.sigmoid(a0_ref[...] + _dot(ta.astype(BF16), a2_ref[...])).astype(BF16)

    gg = jax.nn.sigmoid(_dot(mix(5), g1_ref[...]))
    g_out[...] = _dot(gg.astype(BF16), g2_ref[...]).astype(BF16)

    kkr = k * kk_ref[...]
    ssq = _dot((kkr * kkr).astype(BF16), e_ref[...])
    inv = 1.0 / jnp.maximum(jnp.sqrt(ssq), 1e-12)
    hi, lo = _split_bf16(inv)
    kk_out[...] = (kkr * (_dot(hi, et_ref[...]) + _dot(lo, et_ref[...]))).astype(BF16)


def _row_block_specs(tb, nhalo_blocks):
    per = tb // HALO
    main = pl.BlockSpec((None, tb, D_MODEL), lambda b, i, *_: (b, i, 0))
    prev = pl.BlockSpec((None, HALO, D_MODEL), lambda b, i, *_: (b, jnp.maximum(i * per - 1, 0), 0))
    nxt = pl.BlockSpec((None, HALO, D_MODEL),
                       lambda b, i, *_: (b, jnp.minimum((i + 1) * per, nhalo_blocks - 1), 0))
    return main, prev, nxt


def _const_spec(shape):
    nd = len(shape)
    return pl.BlockSpec(shape, lambda *_: (0,) * nd)


def _rwkv_proj(h, p, tb, tr):
    B, Tp, D = h.shape
    nblk = Tp // tb
    main, prev, nxt = _row_block_specs(tb, Tp // HALO)
    row = lambda w: pl.BlockSpec((None, tb, w), lambda b, i: (b, i, 0))
    consts = [p['norm_mix0'], p['mu'], p['w_rkv'], p['w1'], p['w2'], p['w0'], p['a1'], p['a2'], p['a0'],
              p['g1'], p['g2'], p['k_k'], p['E'], p['ET']]
    out_shape = [jax.ShapeDtypeStruct((B, Tp, D), BF16)] * 5 + [
        jax.ShapeDtypeStruct((B, Tp, 2 * D), BF16), jax.ShapeDtypeStruct((B, Tp, 2 * D), F32)]
    return pl.pallas_call(
        functools.partial(_proj_kernel, tb=tb, tr=tr, nblk=nblk),
        grid=(B, nblk),
        in_specs=[main, prev, nxt] + [_const_spec(c.shape) for c in consts],
        out_specs=[row(D)] * 5 + [row(2 * D), row(2 * D)],
        out_shape=out_shape,
        compiler_params=pltpu.CompilerParams(
            dimension_semantics=("parallel", "arbitrary"), vmem_limit_bytes=VMEM_LIMIT_BYTES),
        name="rwkv_proj",
    )(h, h, h, *consts)


def _wkv_kernel(rf_ref, kf_ref, vf_ref, kkf_ref, af_ref, lwf_ref,
                rb_ref, kb_ref, vb_ref, kkb_ref, ab_ref, lwb_ref,
                ka_ref, tri_ref, of_ref, ob_ref, state_ref):
    c = pl.program_id(1)

    @pl.when(c == 0)
    def _():
        state_ref[...] = jnp.zeros_like(state_ref)

    L = CHUNK
    t_idx = lax.broadcasted_iota(jnp.int32, (L, PAIR), 0)
    lane = lax.broadcasted_iota(jnp.int32, (L, PAIR), 1)
    s_idx = lane % HEAD_SIZE
    head0 = lax.broadcasted_iota(jnp.int32, (1, PAIR), 1) < HEAD_SIZE
    bd_mask = (lax.broadcasted_iota(jnp.int32, (PAIR, PAIR), 0) // HEAD_SIZE
               == lax.broadcasted_iota(jnp.int32, (PAIR, PAIR), 1) // HEAD_SIZE)
    ka = ka_ref[...]

    dirs = (
        (0, rf_ref, kf_ref, vf_ref, kkf_ref, af_ref, lwf_ref, of_ref, s_idx < t_idx, s_idx <= t_idx, L - 1),
        (1, rb_ref, kb_ref, vb_ref, kkb_ref, ab_ref, lwb_ref, ob_ref, s_idx > t_idx, s_idx >= t_idx, 0),
    )
    for d, r_ref, k_ref, v_ref, kk_ref, a_ref, lw_ref, o_ref, strict, incl, last in dirs:
        lw = lw_ref[...]
        hi = lw.astype(BF16)
        rem = lw - hi.astype(F32)
        mid = rem.astype(BF16)
        lo = (rem - mid.astype(F32)).astype(BF16)
        tri = tri_ref[d]
        cum = _dot(tri, hi) + _dot(tri, mid) + _dot(tri, lo)
        cum_last = cum[last:last + 1, :]
        g_incl = jnp.exp(cum)
        g_excl = jnp.exp(cum - lw)
        g_inv = jnp.exp(-cum)
        g_tail = jnp.exp(cum_last - cum)
        g_last = jnp.exp(cum_last)

        kk = kk_ref[...].astype(F32)
        a = a_ref[...].astype(F32)
        kdir = k_ref[...].astype(F32) * (1.0 + (a - 1.0) * ka)
        b = kk * a
        a_t = (-kk * g_excl).astype(BF16)
        r_t = (r_ref[...].astype(F32) * g_incl).astype(BF16)
        b_t = b * g_inv
        k_t = kdir * g_inv
        b_h = (b * g_tail).astype(BF16)
        k_h = (kdir * g_tail).astype(BF16)
        v = v_ref[...]

        for p in range(N_PAIRS):
            sl = slice(p * PAIR, (p + 1) * PAIR)
            s0 = state_ref[d, p]
            ar = jnp.concatenate([a_t[:, sl], r_t[:, sl]], axis=0)
            bp, kp, vp = b_t[:, sl], k_t[:, sl], v[:, sl]
            bk = jnp.concatenate([jnp.where(head0, bp, 0.0), jnp.where(head0, 0.0, bp),
                                  jnp.where(head0, kp, 0.0), jnp.where(head0, 0.0, kp)],
                                 axis=0).astype(BF16)
            a_all = _dot_nt(ar, bk)
            ah = _dot_nt(ar, s0.astype(BF16))
            n_ab = jnp.where(strict, a_all[:L, :PAIR], 0.0)
            a_ak = jnp.where(strict, a_all[:L, PAIR:], 0.0)
            a_rb = jnp.where(incl, a_all[L:, :PAIR], 0.0)
            a_rk = jnp.where(incl, a_all[L:, PAIR:], 0.0)
            v_bd = jnp.concatenate([jnp.where(head0, vp, 0), jnp.where(head0, 0, vp)], axis=0)
            w = ah[:L] + _dot(a_ak.astype(BF16), v_bd)
            pk = n_ab
            steps = int(math.log2(L))
            for step in range(steps):
                pkb = pk.astype(BF16)
                wb = w.astype(BF16)
                if step < steps - 1:
                    rhs = jnp.concatenate(
                        [jnp.concatenate([jnp.where(head0, pkb, 0), jnp.where(head0, wb, 0)], axis=1),
                         jnp.concatenate([jnp.where(head0, 0, pkb), jnp.where(head0, 0, wb)], axis=1)], axis=0)
                    out = _dot(pkb, rhs)
                    pk = out[:, :PAIR]
                    w = w + out[:, PAIR:]
                else:
                    rhs = jnp.concatenate([jnp.where(head0, wb, 0), jnp.where(head0, 0, wb)], axis=0)
                    w = w + _dot(pkb, rhs)
            u = w
            ub = u.astype(BF16)
            uv_bd = jnp.concatenate([jnp.where(head0, ub, 0), jnp.where(head0, 0, ub), v_bd], axis=0)
            a_rbk = jnp.concatenate([a_rb, a_rk], axis=1).astype(BF16)
            o_ref[:, sl] = ah[L:] + _dot(a_rbk, uv_bd)
            upd = _dot_tn(jnp.concatenate([ub, vp], axis=0),
                          jnp.concatenate([b_h[:, sl], k_h[:, sl]], axis=0))
            state_ref[d, p] = s0 * g_last[:, sl] + jnp.where(bd_mask, upd, 0.0)


def _wkv(r, k, v, kk, a, lw, k_a, tri):
    B, Tp, D = r.shape
    n = Tp // CHUNK
    fwd = lambda col: pl.BlockSpec((None, CHUNK, D), lambda b, c: (b, c, col))
    bwd = lambda col: pl.BlockSpec((None, CHUNK, D), lambda b, c: (b, n - 1 - c, col))
    in_specs = ([fwd(0)] * 6 + [bwd(0)] * 4 + [bwd(1), bwd(1)]
                + [_const_spec(k_a.shape), _const_spec(tri.shape)])
    return pl.pallas_call(
        _wkv_kernel,
        grid=(B, n),
        in_specs=in_specs,
        out_specs=[fwd(0), bwd(0)],
        out_shape=[jax.ShapeDtypeStruct((B, Tp, D), F32)] * 2,
        scratch_shapes=[pltpu.VMEM((2, N_PAIRS, PAIR, PAIR), F32)],
        compiler_params=pltpu.CompilerParams(
            dimension_semantics=("parallel", "arbitrary"), vmem_limit_bytes=VMEM_LIMIT_BYTES),
        name="wkv",
    )(r, k, v, kk, a, lw, r, k, v, kk, a, lw, k_a, tri)


def _rwkv_out_kernel(of_ref, ob_ref, r_ref, k_ref, v_ref, g_ref, af_ref, ab_ref, h_ref,
                     rk_ref, ka_ref, gnw_ref, gnb_ref, wo_ref, e_ref, et_ref, out_ref):
    o = of_ref[...] + ob_ref[...]
    inv_n = 1.0 / HEAD_SIZE
    mean = _head_sum_bcast(o, e_ref, et_ref) * inv_n
    dlt = o - mean
    var = _head_sum_bcast(dlt * dlt, e_ref, et_ref) * inv_n
    on = dlt * lax.rsqrt(var + GN_EPS) * gnw_ref[...] + gnb_ref[...]
    r = r_ref[...].astype(F32)
    k = k_ref[...].astype(F32)
    v = v_ref[...].astype(F32)
    a_sum = af_ref[...].astype(F32) + ab_ref[...].astype(F32)
    rk = r * rk_ref[...] * k * (2.0 + (a_sum - 2.0) * ka_ref[...])
    on = on + _head_sum_bcast(rk, e_ref, et_ref) * v
    y = (on * g_ref[...].astype(F32)).astype(BF16)
    out_ref[...] = h_ref[...] + _dot(y, wo_ref[...])


def _rwkv_out(o_f, o_b, r, k, v, g, a, h, p, tb):
    B, Tp, D = h.shape
    row = lambda col=0: pl.BlockSpec((None, tb, D), lambda b, i: (b, i, col))
    consts = [p['r_k'], p['k_a'], p['gn_w'], p['gn_b'], p['w_o'], p['E'], p['ET']]
    return pl.pallas_call(
        _rwkv_out_kernel,
        grid=(B, Tp // tb),
        in_specs=[row()] * 6 + [row(0), row(1), row()] + [_const_spec(c.shape) for c in consts],
        out_specs=row(),
        out_shape=jax.ShapeDtypeStruct((B, Tp, D), F32),
        compiler_params=pltpu.CompilerParams(
            dimension_semantics=("parallel", "parallel"), vmem_limit_bytes=VMEM_LIMIT_BYTES),
        name="rwkv_out",
    )(o_f, o_b, r, k, v, g, a, a, h, *consts)


def _ffn_kernel(h_ref, hp_ref, hn_ref, gain_ref, wact_ref, wlin_ref, cw_ref, cb_ref, wout_ref, gfin_ref,
                out_ref, xw_ref, acc_ref, *, tb, tr, nblk, nff, final_norm):
    i = pl.program_id(1)
    j = pl.program_id(2)

    @pl.when(j == 0)
    def _():
        gain = gain_ref[...]
        x, _, _, _ = _normed_window(h_ref, hp_ref, hn_ref, gain, i, nblk, tb, tr)
        row0 = i * tb
        rp = row0 - HALO + lax.broadcasted_iota(jnp.int32, (HALO, 1), 0)
        xp = jnp.where(jnp.logical_and(rp >= 0, rp < tr), _rmsnorm(hp_ref[...], gain), 0.0)
        rn = row0 + tb + lax.broadcasted_iota(jnp.int32, (HALO, 1), 0)
        xn = jnp.where(jnp.logical_and(i < nblk - 1, rn < tr), _rmsnorm(hn_ref[...], gain), 0.0)
        xw_ref[0:HALO, :] = xp
        xw_ref[HALO:HALO + tb, :] = x
        xw_ref[HALO + tb:, :] = xn
        acc_ref[...] = jnp.zeros_like(acc_ref)

    xw = xw_ref[...]
    u_act = _dot(xw.astype(BF16), wact_ref[...])
    u_lin = _dot(xw[HALO:HALO + tb].astype(BF16), wlin_ref[...])
    cw = cw_ref[...]
    n_win = tb + 2 * HALO
    c = (pltpu.roll(u_act, 1, 0)[HALO:HALO + tb] * cw[0:1, :]
         + u_act[HALO:HALO + tb] * cw[1:2, :]
         + pltpu.roll(u_act, n_win - 1, 0)[HALO:HALO + tb] * cw[2:3, :]
         + cb_ref[...])
    y = (c * jax.nn.sigmoid(c) * u_lin).astype(BF16)
    acc_ref[...] += _dot(y, wout_ref[...])

    @pl.when(j == nff - 1)
    def _():
        res = h_ref[...] + acc_ref[...]
        if final_norm:
            res = _rmsnorm(res, gfin_ref[...])
        out_ref[...] = res


def _ffn(h, gain, w_in, conv_w, conv_b, w_out, g_final, tb, tr, final_norm):
    B, Tp, D = h.shape
    nblk = Tp // tb
    nff = D_FF // FF_BLOCK
    main, prev, nxt = _row_block_specs(tb, Tp // HALO)
    in_specs = [
        main, prev, nxt, _const_spec(gain.shape),
        pl.BlockSpec((D, FF_BLOCK), lambda b, i, j: (0, j)),
        pl.BlockSpec((D, FF_BLOCK), lambda b, i, j: (0, nff + j)),
        pl.BlockSpec((conv_w.shape[0], FF_BLOCK), lambda b, i, j: (0, j)),
        pl.BlockSpec((1, FF_BLOCK), lambda b, i, j: (0, j)),
        pl.BlockSpec((FF_BLOCK, D), lambda b, i, j: (j, 0)),
        _const_spec(g_final.shape),
    ]
    return pl.pallas_call(
        functools.partial(_ffn_kernel, tb=tb, tr=tr, nblk=nblk, nff=nff, final_norm=final_norm),
        grid=(B, nblk, nff),
        in_specs=in_specs,
        out_specs=pl.BlockSpec((None, tb, D), lambda b, i, j: (b, i, 0)),
        out_shape=jax.ShapeDtypeStruct((B, Tp, D), F32),
        scratch_shapes=[pltpu.VMEM((tb + 2 * HALO, D), F32), pltpu.VMEM((tb, D), F32)],
        compiler_params=pltpu.CompilerParams(
            dimension_semantics=("parallel", "parallel", "arbitrary"), vmem_limit_bytes=VMEM_LIMIT_BYTES),
        name="ffn_final" if final_norm else "ffn",
    )(h, h, h, gain, w_in, w_in, conv_w, conv_b, w_out, g_final)


def _fnet_c_kernel(h_ref, gain_ref, cs_ref, y_ref, *, tb, tr):
    i = pl.program_id(1)
    rows = i * tb + lax.broadcasted_iota(jnp.int32, (tb, 1), 0)
    x = jnp.where(rows < tr, _rmsnorm(h_ref[...], gain_ref[...]), 0.0).astype(BF16)
    cs = cs_ref[...]
    for g in range(FNET_GROUPS):
        sl = slice(g * FNET_GROUP_DIM, (g + 1) * FNET_GROUP_DIM)
        y = _dot(x[:, sl], cs)
        y_ref[0, :, sl] = y[:, :FNET_GROUP_DIM].astype(BF16)
        y_ref[1, :, sl] = y[:, FNET_GROUP_DIM:].astype(BF16)


def _fnet_c(h, gain, cs, tb, tr):
    B, Tp, D = h.shape
    return pl.pallas_call(
        functools.partial(_fnet_c_kernel, tb=tb, tr=tr),
        grid=(B, Tp // tb),
        in_specs=[pl.BlockSpec((None, tb, D), lambda b, i: (b, i, 0)),
                  _const_spec(gain.shape), _const_spec(cs.shape)],
        out_specs=pl.BlockSpec((None, 2, tb, D), lambda b, i: (b, 0, i, 0)),
        out_shape=jax.ShapeDtypeStruct((B, 2, Tp, D), BF16),
        compiler_params=pltpu.CompilerParams(
            dimension_semantics=("parallel", "parallel"), vmem_limit_bytes=VMEM_LIMIT_BYTES),
        name="fnet_c",
    )(h, gain, cs)


def _fnet_t_kernel(wt_ref, y_ref, h_ref, wf_ref, out_ref, acc_ref, *, nk):
    kq = pl.program_id(2)

    @pl.when(kq == 0)
    def _():
        acc_ref[...] = jnp.zeros_like(acc_ref)

    acc_ref[...] += _dot(wt_ref[...], y_ref[...])

    @pl.when(kq == nk - 1)
    def _():
        out_ref[...] = h_ref[...] + _dot(acc_ref[...].astype(BF16), wf_ref[...])


def _fnet_t(wt, y, h, w_f, tm, tk):
    B, Tp, D = h.shape
    nk = (2 * Tp) // tk
    return pl.pallas_call(
        functools.partial(_fnet_t_kernel, nk=nk),
        grid=(B, Tp // tm, nk),
        in_specs=[pl.BlockSpec((tm, tk), lambda b, m, q: (m, q)),
                  pl.BlockSpec((None, tk, D), lambda b, m, q: (b, q, 0)),
                  pl.BlockSpec((None, tm, D), lambda b, m, q: (b, m, 0)),
                  _const_spec(w_f.shape)],
        out_specs=pl.BlockSpec((None, tm, D), lambda b, m, q: (b, m, 0)),
        out_shape=jax.ShapeDtypeStruct((B, Tp, D), F32),
        scratch_shapes=[pltpu.VMEM((tm, D), F32)],
        compiler_params=pltpu.CompilerParams(
            dimension_semantics=("parallel", "parallel", "arbitrary"), vmem_limit_bytes=VMEM_LIMIT_BYTES),
        name="fnet_t",
    )(wt, y.reshape(B, 2 * Tp, D), h, w_f)


def _dft_table(n, rows, cols):
    j = lax.broadcasted_iota(jnp.int32, (rows, cols), 0)
    k = lax.broadcasted_iota(jnp.int32, (rows, cols), 1)
    ang = ((j * k) % n).astype(F32) * (2.0 * math.pi / n)
    return jnp.cos(ang), jnp.sin(ang)


def _time_dft_matrix(tr, tp):
    c, s = _dft_table(tr, tp, tp)
    j = lax.broadcasted_iota(jnp.int32, (tp, tp), 0)
    k = lax.broadcasted_iota(jnp.int32, (tp, tp), 1)
    ok = jnp.logical_and(j < tr, k < tr)
    scale = 1.0 / math.sqrt(tr)
    c = jnp.where(ok, c * scale, 0.0)
    s = jnp.where(ok, -s * scale, 0.0)
    return jnp.concatenate([c, s], axis=1).astype(BF16)


def _largest_divisor(n, unit, cap):
    best = unit
    for m in range(unit, cap + 1, unit):
        if n % m == 0:
            best = m
    return best


def _tiles(tp):
    tb = _largest_divisor(tp, 16, 448)
    full_k = 2 * tp <= 4608
    tk = 2 * tp if full_k else _largest_divisor(2 * tp, 128, 2048)
    tm = _largest_divisor(tp, 16, 448 if full_k else 896)
    return tb, tm, tk


def _prepare(meta_tokens, norm_mix, norm_ffn, norm_final, rwkv_mu, rwkv_w_rkv, rwkv_w0, rwkv_w1, rwkv_w2,
             rwkv_a0, rwkv_a1, rwkv_a2, rwkv_g1, rwkv_g2, rwkv_k_k, rwkv_k_a, rwkv_r_k, rwkv_gn_w, rwkv_gn_b,
             rwkv_w_o, fnet_w_o, ffn_w_in, ffn_conv_w, ffn_conv_b, ffn_w_out):
    D = D_MODEL
    row = lambda x: x.reshape(1, -1).astype(F32)

    def lora_pair(w1, w2, w0):
        first = jnp.concatenate([w1[0], w1[1]], axis=1)
        z = jnp.zeros_like(w2[0])
        second = jnp.concatenate([jnp.concatenate([w2[0], z], axis=1),
                                  jnp.concatenate([z, w2[1]], axis=1)], axis=0)
        return first.astype(BF16), second.astype(BF16), jnp.concatenate([w0[0], w0[1]]).reshape(1, 2 * D)

    w1, w2, w0 = lora_pair(rwkv_w1[0], rwkv_w2[0], rwkv_w0[0])
    a1, a2, a0 = lora_pair(rwkv_a1[0], rwkv_a2[0], rwkv_a0[0])
    rank = rwkv_g1.shape[-1]
    g1 = jnp.pad(rwkv_g1[0], ((0, 0), (0, GATE_LORA_PAD - rank))).astype(BF16)
    g2 = jnp.pad(rwkv_g2[0], ((0, GATE_LORA_PAD - rank), (0, 0))).astype(BF16)
    head_of_lane = np.arange(D) // HEAD_SIZE
    e = (head_of_lane[:, None] == np.arange(128)[None, :]).astype(np.float32)
    cc, sc = _dft_table(FNET_GROUP_DIM, FNET_GROUP_DIM, FNET_GROUP_DIM)
    cs = (jnp.concatenate([cc, sc], axis=1) * (1.0 / math.sqrt(FNET_GROUP_DIM))).astype(BF16)
    tri_f = np.tril(np.ones((CHUNK, CHUNK), np.float32))
    return dict(
        meta=meta_tokens,
        norm_mix0=row(norm_mix[0]), norm_mix1=row(norm_mix[1]),
        norm_ffn0=row(norm_ffn[0]), norm_ffn1=row(norm_ffn[1]), norm_final=row(norm_final),
        mu=jnp.pad(rwkv_mu[0], ((0, 2), (0, 0))),
        w_rkv=rwkv_w_rkv[0].astype(BF16), w1=w1, w2=w2, w0=w0, a1=a1, a2=a2, a0=a0, g1=g1, g2=g2,
        k_k=row(rwkv_k_k[0]), k_a=row(rwkv_k_a[0]), r_k=row(rwkv_r_k[0]),
        gn_w=row(rwkv_gn_w[0]), gn_b=row(rwkv_gn_b[0]),
        w_o=rwkv_w_o[0].astype(BF16), w_f=fnet_w_o[0].astype(BF16),
        ffn_w_in=ffn_w_in.astype(BF16), ffn_conv_w=ffn_conv_w, ffn_conv_b=ffn_conv_b.reshape(-1, 1, D_FF),
        ffn_w_out=ffn_w_out.astype(BF16),
        E=jnp.asarray(e, BF16), ET=jnp.asarray(e.T, BF16), cs=cs,
        tri=jnp.asarray(np.stack([tri_f, tri_f.T]), BF16),
    )


def _trunk(x, p):
    B, T, D = x.shape
    tr = T + N_META
    tp = -(-tr // CHUNK) * CHUNK
    tb, tm, tk = _tiles(tp)
    meta = jnp.broadcast_to(p['meta'].astype(x.dtype)[None], (B, N_META, D))
    h = jnp.concatenate([meta, x, jnp.zeros((B, tp - tr, D), x.dtype)], axis=1)

    r, k, v, kk, g, a, lw = _rwkv_proj(h, p, tb, tr)
    o_f, o_b = _wkv(r, k, v, kk, a, lw, p['k_a'], p['tri'])
    h = _rwkv_out(o_f, o_b, r, k, v, g, a, h, p, tb)
    h = _ffn(h, p['norm_ffn0'], p['ffn_w_in'][0], p['ffn_conv_w'][0], p['ffn_conv_b'][0], p['ffn_w_out'][0],
             p['norm_final'], tb, tr, final_norm=False)

    y = _fnet_c(h, p['norm_mix1'], p['cs'], tb, tr)
    h = _fnet_t(_time_dft_matrix(tr, tp), y, h, p['w_f'], tm, tk)
    h = _ffn(h, p['norm_ffn1'], p['ffn_w_in'][1], p['ffn_conv_w'][1], p['ffn_conv_b'][1], p['ffn_w_out'][1],
             p['norm_final'], tb, tr, final_norm=True)
    return h[:, N_META:tr]


def kernel(x_prompt, x_sample, meta_tokens, norm_mix, norm_ffn, norm_final, rwkv_mu, rwkv_w_rkv, rwkv_w0, rwkv_w1, rwkv_w2, rwkv_a0, rwkv_a1, rwkv_a2, rwkv_g1, rwkv_g2, rwkv_k_k, rwkv_k_a, rwkv_r_k, rwkv_gn_w, rwkv_gn_b, rwkv_w_o, fnet_w_o, ffn_w_in, ffn_conv_w, ffn_conv_b, ffn_w_out):
    p = _prepare(meta_tokens, norm_mix, norm_ffn, norm_final, rwkv_mu, rwkv_w_rkv, rwkv_w0, rwkv_w1, rwkv_w2,
                 rwkv_a0, rwkv_a1, rwkv_a2, rwkv_g1, rwkv_g2, rwkv_k_k, rwkv_k_a, rwkv_r_k, rwkv_gn_w,
                 rwkv_gn_b, rwkv_w_o, fnet_w_o, ffn_w_in, ffn_conv_w, ffn_conv_b, ffn_w_out)
    return (_trunk(x_prompt, p), _trunk(x_sample, p))
```

```python
import functools
import math

import jax
import jax.numpy as jnp
import numpy as np
from jax import lax
from jax.experimental import pallas as pl
from jax.experimental.pallas import tpu as pltpu

D_MODEL = 1024
N_META = 16
HEAD_SIZE = 64
N_HEADS = D_MODEL // HEAD_SIZE
GATE_LORA_PAD = 256
LORA2 = 128
D_FF = 2816
FNET_GROUPS = 8
FNET_GROUP_DIM = D_MODEL // FNET_GROUPS
RMS_EPS = 1e-6
GN_EPS = 64e-5

CHUNK = 64
PAIR = 2 * HEAD_SIZE
N_PAIRS = N_HEADS // 2
HALO = 8
FF_BLOCK = D_FF // 2
VMEM_LIMIT_BYTES = 56 * 1024 * 1024

BF16 = jnp.bfloat16
F32 = jnp.float32


def _dot(a, b):
    return jnp.dot(a, b, preferred_element_type=F32)


def _dot_nt(a, b):
    return lax.dot_general(a, b, (((1,), (1,)), ((), ())), preferred_element_type=F32)


def _dot_tn(a, b):
    return lax.dot_general(a, b, (((0,), (0,)), ((), ())), preferred_element_type=F32)


def _rmsnorm(x, gain):
    ms = jnp.mean(x * x, axis=-1, keepdims=True)
    return x * lax.rsqrt(ms + RMS_EPS) * gain


def _split_bf16(x):
    hi = x.astype(BF16)
    lo = (x - hi.astype(F32)).astype(BF16)
    return hi, lo


def _head_sum_bcast(x, e_ref, et_ref):
    s = _dot(x.astype(BF16), e_ref[...])
    hi, lo = _split_bf16(s)
    return _dot(hi, et_ref[...]) + _dot(lo, et_ref[...])


def _normed_window(h_ref, hp_ref, hn_ref, gain, i, nblk, tb, tr):
    row0 = i * tb
    rows = row0 + lax.broadcasted_iota(jnp.int32, (tb, 1), 0)
    valid = rows < tr
    x = jnp.where(valid, _rmsnorm(h_ref[...], gain), 0.0)
    xp = _rmsnorm(hp_ref[...], gain)[HALO - 1:HALO, :]
    xp = jnp.where(i > 0, xp, 0.0)
    xn = _rmsnorm(hn_ref[...], gain)[0:1, :]
    xn = jnp.where(jnp.logical_and(i < nblk - 1, row0 + tb < tr), xn, 0.0)
    return x, xp, xn, valid


def _shift_rows(x, xp, xn, tb):
    ridx = lax.broadcasted_iota(jnp.int32, (tb, 1), 0)
    x_prev = jnp.where(ridx == 0, xp, pltpu.roll(x, 1, 0))
    x_next = jnp.where(ridx == tb - 1, xn, pltpu.roll(x, tb - 1, 0))
    return x_prev, x_next


def _proj_kernel(h_ref, hp_ref, hn_ref, gain_ref, mu_ref, wrkv_ref, w1_ref, w2_ref, w0_ref,
                 a1_ref, a2_ref, a0_ref, g1_ref, g2_ref, kk_ref, e_ref, et_ref,
                 r_out, k_out, v_out, kk_out, g_out, a_out, lw_out, *, tb, tr, nblk):
    i = pl.program_id(1)
    x, xp, xn, valid = _normed_window(h_ref, hp_ref, hn_ref, gain_ref[...], i, nblk, tb, tr)
    x_prev, x_next = _shift_rows(x, xp, xn, tb)
    xx = jnp.where(valid, 0.5 * (x_prev + x_next) - x, 0.0)
    mu = mu_ref[...]

    def mix(j):
        return (x + xx * mu[j:j + 1, :]).astype(BF16)

    r = _dot(mix(0), wrkv_ref[0])
    k = _dot(mix(2), wrkv_ref[1])
    v = _dot(mix(3), wrkv_ref[2])
    r_out[...] = r.astype(BF16)
    k_out[...] = k.astype(BF16)
    v_out[...] = v.astype(BF16)

    tw = jnp.tanh(_dot(mix(1), w1_ref[...]))
    w_lin = w0_ref[...] + _dot(tw.astype(BF16), w2_ref[...])
    lw_out[...] = (-math.exp(-0.5)) * jax.nn.sigmoid(w_lin)

    ta = _dot(mix(4), a1_ref[...])
    a_out[...] = jax.nn.sigmoid(a0_ref[...] + _dot(ta.astype(BF16), a2_ref[...])).astype(BF16)

    gg = jax.nn.sigmoid(_dot(mix(5), g1_ref[...]))
    g_out[...] = _dot(gg.astype(BF16), g2_ref[...]).astype(BF16)

    kkr = k * kk_ref[...]
    ssq = _dot((kkr * kkr).astype(BF16), e_ref[...])
    inv = 1.0 / jnp.maximum(jnp.sqrt(ssq), 1e-12)
    hi, lo = _split_bf16(inv)
    kk_out[...] = (kkr * (_dot(hi, et_ref[...]) + _dot(lo, et_ref[...]))).astype(BF16)


def _row_block_specs(tb, nhalo_blocks):
    per = tb // HALO
    main = pl.BlockSpec((None, tb, D_MODEL), lambda b, i, *_: (b, i, 0))
    prev = pl.BlockSpec((None, HALO, D_MODEL), lambda b, i, *_: (b, jnp.maximum(i * per - 1, 0), 0))
    nxt = pl.BlockSpec((None, HALO, D_MODEL),
                       lambda b, i, *_: (b, jnp.minimum((i + 1) * per, nhalo_blocks - 1), 0))
    return main, prev, nxt


def _const_spec(shape):
    nd = len(shape)
    return pl.BlockSpec(shape, lambda *_: (0,) * nd)


def _rwkv_proj(h, p, tb, tr):
    B, Tp, D = h.shape
    nblk = Tp // tb
    main, prev, nxt = _row_block_specs(tb, Tp // HALO)
    row = lambda w: pl.BlockSpec((None, tb, w), lambda b, i: (b, i, 0))
    consts = [p['norm_mix0'], p['mu'], p['w_rkv'], p['w1'], p['w2'], p['w0'], p['a1'], p['a2'], p['a0'],
              p['g1'], p['g2'], p['k_k'], p['E'], p['ET']]
    out_shape = [jax.ShapeDtypeStruct((B, Tp, D), BF16)] * 5 + [
        jax.ShapeDtypeStruct((B, Tp, 2 * D), BF16), jax.ShapeDtypeStruct((B, Tp, 2 * D), F32)]
    return pl.pallas_call(
        functools.partial(_proj_kernel, tb=tb, tr=tr, nblk=nblk),
        grid=(B, nblk),
        in_specs=[main, prev, nxt] + [_const_spec(c.shape) for c in consts],
        out_specs=[row(D)] * 5 + [row(2 * D), row(2 * D)],
        out_shape=out_shape,
        compiler_params=pltpu.CompilerParams(
            dimension_semantics=("parallel", "arbitrary"), vmem_limit_bytes=VMEM_LIMIT_BYTES),
        name="rwkv_proj",
    )(h, h, h, *consts)


def _wkv_kernel(rf_ref, kf_ref, vf_ref, kkf_ref, af_ref, lwf_ref,
                rb_ref, kb_ref, vb_ref, kkb_ref, ab_ref, lwb_ref,
                ka_ref, tri_ref, of_ref, ob_ref, state_ref):
    c = pl.program_id(1)

    @pl.when(c == 0)
    def _():
        state_ref[...] = jnp.zeros_like(state_ref)

    L = CHUNK
    t_idx = lax.broadcasted_iota(jnp.int32, (L, PAIR), 0)
    lane = lax.broadcasted_iota(jnp.int32, (L, PAIR), 1)
    s_idx = lane % HEAD_SIZE
    head0 = lax.broadcasted_iota(jnp.int32, (1, PAIR), 1) < HEAD_SIZE
    bd_mask = (lax.broadcasted_iota(jnp.int32, (PAIR, PAIR), 0) // HEAD_SIZE
               == lax.broadcasted_iota(jnp.int32, (PAIR, PAIR), 1) // HEAD_SIZE)
    ka = ka_ref[...]

    dirs = (
        (0, rf_ref, kf_ref, vf_ref, kkf_ref, af_ref, lwf_ref, of_ref, s_idx < t_idx, s_idx <= t_idx, L - 1),
        (1, rb_ref, kb_ref, vb_ref, kkb_ref, ab_ref, lwb_ref, ob_ref, s_idx > t_idx, s_idx >= t_idx, 0),
    )
    chains = []
    for d, r_ref, k_ref, v_ref, kk_ref, a_ref, lw_ref, o_ref, strict, incl, last in dirs:
        lw = lw_ref[...]
        hi = lw.astype(BF16)
        rem = lw - hi.astype(F32)
        mid = rem.astype(BF16)
        lo = (rem - mid.astype(F32)).astype(BF16)
        tri = tri_ref[d]
        cum = _dot(tri, hi) + _dot(tri, mid) + _dot(tri, lo)
        cum_last = cum[last:last + 1, :]
        g_incl = jnp.exp(cum)
        g_excl = jnp.exp(cum - lw)
        g_inv = jnp.exp(-cum)
        g_tail = jnp.exp(cum_last - cum)
        g_last = jnp.exp(cum_last)

        kk = kk_ref[...].astype(F32)
        a = a_ref[...].astype(F32)
        kdir = k_ref[...].astype(F32) * (1.0 + (a - 1.0) * ka)
        b = kk * a
        a_t = (-kk * g_excl).astype(BF16)
        r_t = (r_ref[...].astype(F32) * g_incl).astype(BF16)
        b_t = b * g_inv
        k_t = kdir * g_inv
        b_h = (b * g_tail).astype(BF16)
        k_h = (kdir * g_tail).astype(BF16)
        v = v_ref[...]

        for p in range(N_PAIRS):
            sl = slice(p * PAIR, (p + 1) * PAIR)
            bp, kp, vp = b_t[:, sl], k_t[:, sl], v[:, sl]
            chains.append(dict(
                d=d, p=p, sl=sl, o_ref=o_ref, strict=strict, incl=incl, vp=vp,
                ar=jnp.concatenate([a_t[:, sl], r_t[:, sl]], axis=0),
                bk=jnp.concatenate([jnp.where(head0, bp, 0.0), jnp.where(head0, 0.0, bp),
                                    jnp.where(head0, kp, 0.0), jnp.where(head0, 0.0, kp)],
                                   axis=0).astype(BF16),
                v_bd=jnp.concatenate([jnp.where(head0, vp, 0), jnp.where(head0, 0, vp)], axis=0),
                bk_h=jnp.concatenate([b_h[:, sl], k_h[:, sl]], axis=0),
                g_last=g_last[:, sl]))

    for ch in chains:
        ch['s0'] = state_ref[ch['d'], ch['p']]
        ch['a_all'] = _dot_nt(ch['ar'], ch['bk'])
        ch['ah'] = _dot_nt(ch['ar'], ch['s0'].astype(BF16))
    for ch in chains:
        a_all, strict, incl = ch['a_all'], ch['strict'], ch['incl']
        ch['pk'] = jnp.where(strict, a_all[:L, :PAIR], 0.0)
        a_ak = jnp.where(strict, a_all[:L, PAIR:], 0.0).astype(BF16)
        ch['a_rbk'] = jnp.concatenate([jnp.where(incl, a_all[L:, :PAIR], 0.0),
                                       jnp.where(incl, a_all[L:, PAIR:], 0.0)], axis=1).astype(BF16)
        ch['w'] = ch['ah'][:L] + _dot(a_ak, ch['v_bd'])
    steps = int(math.log2(L))
    for step in range(steps):
        for ch in chains:
            pkb = ch['pk'].astype(BF16)
            wb = ch['w'].astype(BF16)
            if step < steps - 1:
                rhs = jnp.concatenate(
                    [jnp.concatenate([jnp.where(head0, pkb, 0), jnp.where(head0, wb, 0)], axis=1),
                     jnp.concatenate([jnp.where(head0, 0, pkb), jnp.where(head0, 0, wb)], axis=1)], axis=0)
                out = _dot(pkb, rhs)
                ch['pk'] = out[:, :PAIR]
                ch['w'] = ch['w'] + out[:, PAIR:]
            else:
                rhs = jnp.concatenate([jnp.where(head0, wb, 0), jnp.where(head0, 0, wb)], axis=0)
                ch['w'] = ch['w'] + _dot(pkb, rhs)
    for ch in chains:
        ub = ch['w'].astype(BF16)
        uv_bd = jnp.concatenate([jnp.where(head0, ub, 0), jnp.where(head0, 0, ub), ch['v_bd']], axis=0)
        ch['o_ref'][:, ch['sl']] = ch['ah'][L:] + _dot(ch['a_rbk'], uv_bd)
        upd = _dot_tn(jnp.concatenate([ub, ch['vp']], axis=0), ch['bk_h'])
        state_ref[ch['d'], ch['p']] = ch['s0'] * ch['g_last'] + jnp.where(bd_mask, upd, 0.0)


def _wkv(r, k, v, kk, a, lw, k_a, tri):
    B, Tp, D = r.shape
    n = Tp // CHUNK
    fwd = lambda col: pl.BlockSpec((None, CHUNK, D), lambda b, c: (b, c, col))
    bwd = lambda col: pl.BlockSpec((None, CHUNK, D), lambda b, c: (b, n - 1 - c, col))
    in_specs = ([fwd(0)] * 6 + [bwd(0)] * 4 + [bwd(1), bwd(1)]
                + [_const_spec(k_a.shape), _const_spec(tri.shape)])
    return pl.pallas_call(
        _wkv_kernel,
        grid=(B, n),
        in_specs=in_specs,
        out_specs=[fwd(0), bwd(0)],
        out_shape=[jax.ShapeDtypeStruct((B, Tp, D), F32)] * 2,
        scratch_shapes=[pltpu.VMEM((2, N_PAIRS, PAIR, PAIR), F32)],
        compiler_params=pltpu.CompilerParams(
            dimension_semantics=("parallel", "arbitrary"), vmem_limit_bytes=VMEM_LIMIT_BYTES),
        name="wkv",
    )(r, k, v, kk, a, lw, r, k, v, kk, a, lw, k_a, tri)


def _rwkv_out_kernel(of_ref, ob_ref, r_ref, k_ref, v_ref, g_ref, af_ref, ab_ref, h_ref,
                     rk_ref, ka_ref, gnw_ref, gnb_ref, wo_ref, e_ref, et_ref, out_ref):
    o = of_ref[...] + ob_ref[...]
    inv_n = 1.0 / HEAD_SIZE
    mean = _head_sum_bcast(o, e_ref, et_ref) * inv_n
    dlt = o - mean
    var = _head_sum_bcast(dlt * dlt, e_ref, et_ref) * inv_n
    on = dlt * lax.rsqrt(var + GN_EPS) * gnw_ref[...] + gnb_ref[...]
    r = r_ref[...].astype(F32)
    k = k_ref[...].astype(F32)
    v = v_ref[...].astype(F32)
    a_sum = af_ref[...].astype(F32) + ab_ref[...].astype(F32)
    rk = r * rk_ref[...] * k * (2.0 + (a_sum - 2.0) * ka_ref[...])
    on = on + _head_sum_bcast(rk, e_ref, et_ref) * v
    y = (on * g_ref[...].astype(F32)).astype(BF16)
    out_ref[...] = h_ref[...] + _dot(y, wo_ref[...])


def _rwkv_out(o_f, o_b, r, k, v, g, a, h, p, tb):
    B, Tp, D = h.shape
    row = lambda col=0: pl.BlockSpec((None, tb, D), lambda b, i: (b, i, col))
    consts = [p['r_k'], p['k_a'], p['gn_w'], p['gn_b'], p['w_o'], p['E'], p['ET']]
    return pl.pallas_call(
        _rwkv_out_kernel,
        grid=(B, Tp // tb),
        in_specs=[row()] * 6 + [row(0), row(1), row()] + [_const_spec(c.shape) for c in consts],
        out_specs=row(),
        out_shape=jax.ShapeDtypeStruct((B, Tp, D), F32),
        compiler_params=pltpu.CompilerParams(
            dimension_semantics=("parallel", "parallel"), vmem_limit_bytes=VMEM_LIMIT_BYTES),
        name="rwkv_out",
    )(o_f, o_b, r, k, v, g, a, a, h, *consts)


def _ffn_kernel(h_ref, hp_ref, hn_ref, gain_ref, wact_ref, wlin_ref, cw_ref, cb_ref, wout_ref, gfin_ref,
                out_ref, xw_ref, acc_ref, *, tb, tr, nblk, nff, final_norm):
    i = pl.program_id(1)
    j = pl.program_id(2)

    @pl.when(j == 0)
    def _():
        gain = gain_ref[...]
        x, _, _, _ = _normed_window(h_ref, hp_ref, hn_ref, gain, i, nblk, tb, tr)
        row0 = i * tb
        rp = row0 - HALO + lax.broadcasted_iota(jnp.int32, (HALO, 1), 0)
        xp = jnp.where(jnp.logical_and(rp >= 0, rp < tr), _rmsnorm(hp_ref[...], gain), 0.0)
        rn = row0 + tb + lax.broadcasted_iota(jnp.int32, (HALO, 1), 0)
        xn = jnp.where(jnp.logical_and(i < nblk - 1, rn < tr), _rmsnorm(hn_ref[...], gain), 0.0)
        xw_ref[0:HALO, :] = xp
        xw_ref[HALO:HALO + tb, :] = x
        xw_ref[HALO + tb:, :] = xn
        acc_ref[...] = jnp.zeros_like(acc_ref)

    xw = xw_ref[...]
    u_act = _dot(xw.astype(BF16), wact_ref[...])
    u_lin = _dot(xw[HALO:HALO + tb].astype(BF16), wlin_ref[...])
    cw = cw_ref[...]
    n_win = tb + 2 * HALO
    c = (pltpu.roll(u_act, 1, 0)[HALO:HALO + tb] * cw[0:1, :]
         + u_act[HALO:HALO + tb] * cw[1:2, :]
         + pltpu.roll(u_act, n_win - 1, 0)[HALO:HALO + tb] * cw[2:3, :]
         + cb_ref[...])
    y = (c * jax.nn.sigmoid(c) * u_lin).astype(BF16)
    acc_ref[...] += _dot(y, wout_ref[...])

    @pl.when(j == nff - 1)
    def _():
        res = h_ref[...] + acc_ref[...]
        if final_norm:
            res = _rmsnorm(res, gfin_ref[...])
        out_ref[...] = res


def _ffn(h, gain, w_in, conv_w, conv_b, w_out, g_final, tb, tr, final_norm):
    B, Tp, D = h.shape
    nblk = Tp // tb
    nff = D_FF // FF_BLOCK
    main, prev, nxt = _row_block_specs(tb, Tp // HALO)
    in_specs = [
        main, prev, nxt, _const_spec(gain.shape),
        pl.BlockSpec((D, FF_BLOCK), lambda b, i, j: (0, j)),
        pl.BlockSpec((D, FF_BLOCK), lambda b, i, j: (0, nff + j)),
        pl.BlockSpec((conv_w.shape[0], FF_BLOCK), lambda b, i, j: (0, j)),
        pl.BlockSpec((1, FF_BLOCK), lambda b, i, j: (0, j)),
        pl.BlockSpec((FF_BLOCK, D), lambda b, i, j: (j, 0)),
        _const_spec(g_final.shape),
    ]
    return pl.pallas_call(
        functools.partial(_ffn_kernel, tb=tb, tr=tr, nblk=nblk, nff=nff, final_norm=final_norm),
        grid=(B, nblk, nff),
        in_specs=in_specs,
        out_specs=pl.BlockSpec((None, tb, D), lambda b, i, j: (b, i, 0)),
        out_shape=jax.ShapeDtypeStruct((B, Tp, D), F32),
        scratch_shapes=[pltpu.VMEM((tb + 2 * HALO, D), F32), pltpu.VMEM((tb, D), F32)],
        compiler_params=pltpu.CompilerParams(
            dimension_semantics=("parallel", "parallel", "arbitrary"), vmem_limit_bytes=VMEM_LIMIT_BYTES),
        name="ffn_final" if final_norm else "ffn",
    )(h, h, h, gain, w_in, w_in, conv_w, conv_b, w_out, g_final)


def _fnet_c_kernel(h_ref, gain_ref, cs_ref, y_ref, *, tb, tr):
    i = pl.program_id(1)
    rows = i * tb + lax.broadcasted_iota(jnp.int32, (tb, 1), 0)
    x = jnp.where(rows < tr, _rmsnorm(h_ref[...], gain_ref[...]), 0.0).astype(BF16)
    cs = cs_ref[...]
    for g in range(FNET_GROUPS):
        sl = slice(g * FNET_GROUP_DIM, (g + 1) * FNET_GROUP_DIM)
        y = _dot(x[:, sl], cs)
        y_ref[0, :, sl] = y[:, :FNET_GROUP_DIM].astype(BF16)
        y_ref[1, :, sl] = y[:, FNET_GROUP_DIM:].astype(BF16)


def _fnet_c(h, gain, cs, tb, tr):
    B, Tp, D = h.shape
    return pl.pallas_call(
        functools.partial(_fnet_c_kernel, tb=tb, tr=tr),
        grid=(B, Tp // tb),
        in_specs=[pl.BlockSpec((None, tb, D), lambda b, i: (b, i, 0)),
                  _const_spec(gain.shape), _const_spec(cs.shape)],
        out_specs=pl.BlockSpec((None, 2, tb, D), lambda b, i: (b, 0, i, 0)),
        out_shape=jax.ShapeDtypeStruct((B, 2, Tp, D), BF16),
        compiler_params=pltpu.CompilerParams(
            dimension_semantics=("parallel", "parallel"), vmem_limit_bytes=VMEM_LIMIT_BYTES),
        name="fnet_c",
    )(h, gain, cs)


def _fnet_t_kernel(wt_ref, y_ref, h_ref, wf_ref, out_ref, acc_ref, *, nk):
    kq = pl.program_id(2)

    @pl.when(kq == 0)
    def _():
        acc_ref[...] = jnp.zeros_like(acc_ref)

    acc_ref[...] += _dot(wt_ref[...], y_ref[...])

    @pl.when(kq == nk - 1)
    def _():
        out_ref[...] = h_ref[...] + _dot(acc_ref[...].astype(BF16), wf_ref[...])


def _fnet_t(wt, y, h, w_f, tm, tk):
    B, Tp, D = h.shape
    nk = (2 * Tp) // tk
    return pl.pallas_call(
        functools.partial(_fnet_t_kernel, nk=nk),
        grid=(B, Tp // tm, nk),
        in_specs=[pl.BlockSpec((tm, tk), lambda b, m, q: (m, q)),
                  pl.BlockSpec((None, tk, D), lambda b, m, q: (b, q, 0)),
                  pl.BlockSpec((None, tm, D), lambda b, m, q: (b, m, 0)),
                  _const_spec(w_f.shape)],
        out_specs=pl.BlockSpec((None, tm, D), lambda b, m, q: (b, m, 0)),
        out_shape=jax.ShapeDtypeStruct((B, Tp, D), F32),
        scratch_shapes=[pltpu.VMEM((tm, D), F32)],
        compiler_params=pltpu.CompilerParams(
            dimension_semantics=("parallel", "parallel", "arbitrary"), vmem_limit_bytes=VMEM_LIMIT_BYTES),
        name="fnet_t",
    )(wt, y.reshape(B, 2 * Tp, D), h, w_f)


def _dft_table(n, rows, cols):
    j = lax.broadcasted_iota(jnp.int32, (rows, cols), 0)
    k = lax.broadcasted_iota(jnp.int32, (rows, cols), 1)
    ang = ((j * k) % n).astype(F32) * (2.0 * math.pi / n)
    return jnp.cos(ang), jnp.sin(ang)


def _time_dft_matrix(tr, tp):
    c, s = _dft_table(tr, tp, tp)
    j = lax.broadcasted_iota(jnp.int32, (tp, tp), 0)
    k = lax.broadcasted_iota(jnp.int32, (tp, tp), 1)
    ok = jnp.logical_and(j < tr, k < tr)
    scale = 1.0 / math.sqrt(tr)
    c = jnp.where(ok, c * scale, 0.0)
    s = jnp.where(ok, -s * scale, 0.0)
    return jnp.concatenate([c, s], axis=1).astype(BF16)


def _largest_divisor(n, unit, cap):
    best = unit
    for m in range(unit, cap + 1, unit):
        if n % m == 0:
            best = m
    return best


BF16_ROWS = 16
LANES = 128
ROW_BLOCK_CAP = 448
DFT_FULL_K_CAP = 4608
DFT_K_BLOCK_CAP = 2048
DFT_M_BLOCK_CAP = 896


def _tiles(tp):
    tb = _largest_divisor(tp, BF16_ROWS, ROW_BLOCK_CAP)
    full_k = 2 * tp <= DFT_FULL_K_CAP
    tk = 2 * tp if full_k else _largest_divisor(2 * tp, LANES, DFT_K_BLOCK_CAP)
    tm = _largest_divisor(tp, BF16_ROWS, ROW_BLOCK_CAP if full_k else DFT_M_BLOCK_CAP)
    return tb, tm, tk


def _prepare(meta_tokens, norm_mix, norm_ffn, norm_final, rwkv_mu, rwkv_w_rkv, rwkv_w0, rwkv_w1, rwkv_w2,
             rwkv_a0, rwkv_a1, rwkv_a2, rwkv_g1, rwkv_g2, rwkv_k_k, rwkv_k_a, rwkv_r_k, rwkv_gn_w, rwkv_gn_b,
             rwkv_w_o, fnet_w_o, ffn_w_in, ffn_conv_w, ffn_conv_b, ffn_w_out):
    D = D_MODEL
    row = lambda x: x.reshape(1, -1).astype(F32)

    def lora_pair(w1, w2, w0):
        first = jnp.concatenate([w1[0], w1[1]], axis=1)
        z = jnp.zeros_like(w2[0])
        second = jnp.concatenate([jnp.concatenate([w2[0], z], axis=1),
                                  jnp.concatenate([z, w2[1]], axis=1)], axis=0)
        return first.astype(BF16), second.astype(BF16), jnp.concatenate([w0[0], w0[1]]).reshape(1, 2 * D)

    w1, w2, w0 = lora_pair(rwkv_w1[0], rwkv_w2[0], rwkv_w0[0])
    a1, a2, a0 = lora_pair(rwkv_a1[0], rwkv_a2[0], rwkv_a0[0])
    rank = rwkv_g1.shape[-1]
    g1 = jnp.pad(rwkv_g1[0], ((0, 0), (0, GATE_LORA_PAD - rank))).astype(BF16)
    g2 = jnp.pad(rwkv_g2[0], ((0, GATE_LORA_PAD - rank), (0, 0))).astype(BF16)
    head_of_lane = np.arange(D) // HEAD_SIZE
    e = (head_of_lane[:, None] == np.arange(128)[None, :]).astype(np.float32)
    cc, sc = _dft_table(FNET_GROUP_DIM, FNET_GROUP_DIM, FNET_GROUP_DIM)
    cs = (jnp.concatenate([cc, sc], axis=1) * (1.0 / math.sqrt(FNET_GROUP_DIM))).astype(BF16)
    tri_f = np.tril(np.ones((CHUNK, CHUNK), np.float32))
    return dict(
        meta=meta_tokens,
        norm_mix0=row(norm_mix[0]), norm_mix1=row(norm_mix[1]),
        norm_ffn0=row(norm_ffn[0]), norm_ffn1=row(norm_ffn[1]), norm_final=row(norm_final),
        mu=jnp.pad(rwkv_mu[0], ((0, 2), (0, 0))),
        w_rkv=rwkv_w_rkv[0].astype(BF16), w1=w1, w2=w2, w0=w0, a1=a1, a2=a2, a0=a0, g1=g1, g2=g2,
        k_k=row(rwkv_k_k[0]), k_a=row(rwkv_k_a[0]), r_k=row(rwkv_r_k[0]),
        gn_w=row(rwkv_gn_w[0]), gn_b=row(rwkv_gn_b[0]),
        w_o=rwkv_w_o[0].astype(BF16), w_f=fnet_w_o[0].astype(BF16),
        ffn_w_in=ffn_w_in.astype(BF16), ffn_conv_w=ffn_conv_w, ffn_conv_b=ffn_conv_b.reshape(-1, 1, D_FF),
        ffn_w_out=ffn_w_out.astype(BF16),
        E=jnp.asarray(e, BF16), ET=jnp.asarray(e.T, BF16), cs=cs,
        tri=jnp.asarray(np.stack([tri_f, tri_f.T]), BF16),
    )


def _trunk(x, p):
    B, T, D = x.shape
    tr = T + N_META
    tp = -(-tr // CHUNK) * CHUNK
    tb, tm, tk = _tiles(tp)
    meta = jnp.broadcast_to(p['meta'].astype(x.dtype)[None], (B, N_META, D))
    h = jnp.concatenate([meta, x, jnp.zeros((B, tp - tr, D), x.dtype)], axis=1)

    r, k, v, kk, g, a, lw = _rwkv_proj(h, p, tb, tr)
    o_f, o_b = _wkv(r, k, v, kk, a, lw, p['k_a'], p['tri'])
    h = _rwkv_out(o_f, o_b, r, k, v, g, a, h, p, tb)
    h = _ffn(h, p['norm_ffn0'], p['ffn_w_in'][0], p['ffn_conv_w'][0], p['ffn_conv_b'][0], p['ffn_w_out'][0],
             p['norm_final'], tb, tr, final_norm=False)

    y = _fnet_c(h, p['norm_mix1'], p['cs'], tb, tr)
    h = _fnet_t(_time_dft_matrix(tr, tp), y, h, p['w_f'], tm, tk)
    h = _ffn(h, p['norm_ffn1'], p['ffn_w_in'][1], p['ffn_conv_w'][1], p['ffn_conv_b'][1], p['ffn_w_out'][1],
             p['norm_final'], tb, tr, final_norm=True)
    return h[:, N_META:tr]


def kernel(x_prompt, x_sample, meta_tokens, norm_mix, norm_ffn, norm_final, rwkv_mu, rwkv_w_rkv, rwkv_w0, rwkv_w1, rwkv_w2, rwkv_a0, rwkv_a1, rwkv_a2, rwkv_g1, rwkv_g2, rwkv_k_k, rwkv_k_a, rwkv_r_k, rwkv_gn_w, rwkv_gn_b, rwkv_w_o, fnet_w_o, ffn_w_in, ffn_conv_w, ffn_conv_b, ffn_w_out):
    p = _prepare(meta_tokens, norm_mix, norm_ffn, norm_final, rwkv_mu, rwkv_w_rkv, rwkv_w0, rwkv_w1, rwkv_w2,
                 rwkv_a0, rwkv_a1, rwkv_a2, rwkv_g1, rwkv_g2, rwkv_k_k, rwkv_k_a, rwkv_r_k, rwkv_gn_w,
                 rwkv_gn_b, rwkv_w_o, fnet_w_o, ffn_w_in, ffn_conv_w, ffn_conv_b, ffn_w_out)
    return (_trunk(x_prompt, p), _trunk(x_sample, p))
```

```python
import functools
import math

import jax
import jax.numpy as jnp
import numpy as np
from jax import lax
from jax.experimental import pallas as pl
from jax.experimental.pallas import tpu as pltpu

D_MODEL = 1024
N_META = 16
HEAD_SIZE = 64
N_HEADS = D_MODEL // HEAD_SIZE
GATE_LORA_PAD = 256
LORA2 = 128
D_FF = 2816
FNET_GROUPS = 8
FNET_GROUP_DIM = D_MODEL // FNET_GROUPS
RMS_EPS = 1e-6
GN_EPS = 64e-5

CHUNK = 64
PAIR = 2 * HEAD_SIZE
N_PAIRS = N_HEADS // 2
HALO = 8
FF_CHUNK = 512
META_SHIFT = N_META
NEXT_ROWS = 32
VMEM_LIMIT_BYTES = 56 * 1024 * 1024

BF16 = jnp.bfloat16
F32 = jnp.float32


def _dot(a, b):
    return jnp.dot(a, b, preferred_element_type=F32)


def _dot_nt(a, b):
    return lax.dot_general(a, b, (((1,), (1,)), ((), ())), preferred_element_type=F32)


def _dot_tn(a, b):
    return lax.dot_general(a, b, (((0,), (0,)), ((), ())), preferred_element_type=F32)


def _rmsnorm(x, gain):
    ms = jnp.mean(x * x, axis=-1, keepdims=True)
    return x * lax.rsqrt(ms + RMS_EPS) * gain


def _split_bf16(x):
    hi = x.astype(BF16)
    lo = (x - hi.astype(F32)).astype(BF16)
    return hi, lo


def _head_sum_bcast(x, e_ref, et_ref):
    s = _dot(x.astype(BF16), e_ref[...])
    hi, lo = _split_bf16(s)
    return _dot(hi, et_ref[...]) + _dot(lo, et_ref[...])


def _normed_window(h_ref, hp_ref, hn_ref, gain, i, nblk, tb, tr):
    row0 = i * tb
    rows = row0 + lax.broadcasted_iota(jnp.int32, (tb, 1), 0)
    valid = rows < tr
    x = jnp.where(valid, _rmsnorm(h_ref[...], gain), 0.0)
    xp = _rmsnorm(hp_ref[...], gain)[HALO - 1:HALO, :]
    xp = jnp.where(i > 0, xp, 0.0)
    xn = _rmsnorm(hn_ref[...], gain)[0:1, :]
    xn = jnp.where(jnp.logical_and(i < nblk - 1, row0 + tb < tr), xn, 0.0)
    return x, xp, xn, valid


def _shift_rows(x, xp, xn, tb):
    ridx = lax.broadcasted_iota(jnp.int32, (tb, 1), 0)
    x_prev = jnp.where(ridx == 0, xp, pltpu.roll(x, 1, 0))
    x_next = jnp.where(ridx == tb - 1, xn, pltpu.roll(x, tb - 1, 0))
    return x_prev, x_next


def _proj_kernel(h_ref, hp_ref, hn_ref, gain_ref, mu_ref, wrkv_ref, w1_ref, w2_ref, w0_ref,
                 a1_ref, a2_ref, a0_ref, g1_ref, g2_ref, kk_ref, e_ref, et_ref,
                 r_out, k_out, v_out, kk_out, g_out, a_out, lw_out, *, tb, tr, nblk):
    i = pl.program_id(1)
    x, xp, xn, valid = _normed_window(h_ref, hp_ref, hn_ref, gain_ref[...], i, nblk, tb, tr)
    x_prev, x_next = _shift_rows(x, xp, xn, tb)
    xx = jnp.where(valid, 0.5 * (x_prev + x_next) - x, 0.0)
    mu = mu_ref[...]

    def mix(j):
        return (x + xx * mu[j:j + 1, :]).astype(BF16)

    r = _dot(mix(0), wrkv_ref[0])
    k = _dot(mix(2), wrkv_ref[1])
    v = _dot(mix(3), wrkv_ref[2])
    r_out[...] = r.astype(BF16)
    k_out[...] = k.astype(BF16)
    v_out[...] = v.astype(BF16)

    tw = jnp.tanh(_dot(mix(1), w1_ref[...]))
    w_lin = w0_ref[...] + _dot(tw.astype(BF16), w2_ref[...])
    lw_out[...] = (-math.exp(-0.5)) * jax.nn.sigmoid(w_lin)

    ta = _dot(mix(4), a1_ref[...])
    a_out[...] = jax.nn.sigmoid(a0_ref[...] + _dot(ta.astype(BF16), a2_ref[...])).astype(BF16)

    gg = jax.nn.sigmoid(_dot(mix(5), g1_ref[...]))
    g_out[...] = _dot(gg.astype(BF16), g2_ref[...]).astype(BF16)

    kkr = k * kk_ref[...]
    ssq = _dot((kkr * kkr).astype(BF16), e_ref[...])
    inv = 1.0 / jnp.maximum(jnp.sqrt(ssq), 1e-12)
    hi, lo = _split_bf16(inv)
    kk_out[...] = (kkr * (_dot(hi, et_ref[...]) + _dot(lo, et_ref[...]))).astype(BF16)


def _row_block_specs(tb, nhalo_blocks):
    per = tb // HALO
    main = pl.BlockSpec((None, tb, D_MODEL), lambda b, i, *_: (b, i, 0))
    prev = pl.BlockSpec((None, HALO, D_MODEL), lambda b, i, *_: (b, jnp.maximum(i * per - 1, 0), 0))
    nxt = pl.BlockSpec((None, HALO, D_MODEL),
                       lambda b, i, *_: (b, jnp.minimum((i + 1) * per, nhalo_blocks - 1), 0))
    return main, prev, nxt


def _const_spec(shape):
    nd = len(shape)
    return pl.BlockSpec(shape, lambda *_: (0,) * nd)


def _rwkv_proj(h, p, tb, tr):
    B, Tp, D = h.shape
    nblk = Tp // tb
    main, prev, nxt = _row_block_specs(tb, Tp // HALO)
    row = lambda w: pl.BlockSpec((None, tb, w), lambda b, i: (b, i, 0))
    consts = [p['norm_mix0'], p['mu'], p['w_rkv'], p['w1'], p['w2'], p['w0'], p['a1'], p['a2'], p['a0'],
              p['g1'], p['g2'], p['k_k'], p['E'], p['ET']]
    out_shape = [jax.ShapeDtypeStruct((B, Tp, D), BF16)] * 5 + [
        jax.ShapeDtypeStruct((B, Tp, 2 * D), BF16), jax.ShapeDtypeStruct((B, Tp, 2 * D), F32)]
    return pl.pallas_call(
        functools.partial(_proj_kernel, tb=tb, tr=tr, nblk=nblk),
        grid=(B, nblk),
        in_specs=[main, prev, nxt] + [_const_spec(c.shape) for c in consts],
        out_specs=[row(D)] * 5 + [row(2 * D), row(2 * D)],
        out_shape=out_shape,
        compiler_params=pltpu.CompilerParams(
            dimension_semantics=("parallel", "arbitrary"), vmem_limit_bytes=VMEM_LIMIT_BYTES),
        name="rwkv_proj",
    )(h, h, h, *consts)


def _wkv_kernel(rf_ref, kf_ref, vf_ref, kkf_ref, af_ref, lwf_ref,
                rb_ref, kb_ref, vb_ref, kkb_ref, ab_ref, lwb_ref,
                ka_ref, tri_ref, of_ref, ob_ref, state_ref):
    c = pl.program_id(1)

    @pl.when(c == 0)
    def _():
        state_ref[...] = jnp.zeros_like(state_ref)

    L = CHUNK
    t_idx = lax.broadcasted_iota(jnp.int32, (L, PAIR), 0)
    lane = lax.broadcasted_iota(jnp.int32, (L, PAIR), 1)
    s_idx = lane % HEAD_SIZE
    head0 = lax.broadcasted_iota(jnp.int32, (1, PAIR), 1) < HEAD_SIZE
    bd_mask = (lax.broadcasted_iota(jnp.int32, (PAIR, PAIR), 0) // HEAD_SIZE
               == lax.broadcasted_iota(jnp.int32, (PAIR, PAIR), 1) // HEAD_SIZE)
    ka = ka_ref[...]

    dirs = (
        (0, rf_ref, kf_ref, vf_ref, kkf_ref, af_ref, lwf_ref, of_ref, s_idx < t_idx, s_idx <= t_idx, L - 1),
        (1, rb_ref, kb_ref, vb_ref, kkb_ref, ab_ref, lwb_ref, ob_ref, s_idx > t_idx, s_idx >= t_idx, 0),
    )
    chains = []
    for d, r_ref, k_ref, v_ref, kk_ref, a_ref, lw_ref, o_ref, strict, incl, last in dirs:
        lw = lw_ref[...]
        hi, lo = _split_bf16(lw)
        tri = tri_ref[d]
        cum = _dot(tri, hi) + _dot(tri, lo)
        cum_last = cum[last:last + 1, :]
        g_incl = jnp.exp(cum)
        g_excl = jnp.exp(cum - lw)
        g_inv = jnp.exp(-cum)
        g_tail = jnp.exp(cum_last - cum)
        g_last = jnp.exp(cum_last)

        kk = kk_ref[...].astype(F32)
        a = a_ref[...].astype(F32)
        kdir = k_ref[...].astype(F32) * (1.0 + (a - 1.0) * ka)
        b = kk * a
        a_t = (-kk * g_excl).astype(BF16)
        r_t = (r_ref[...].astype(F32) * g_incl).astype(BF16)
        b_t = b * g_inv
        k_t = kdir * g_inv
        b_h = (b * g_tail).astype(BF16)
        k_h = (kdir * g_tail).astype(BF16)
        v = v_ref[...]

        for p in range(N_PAIRS):
            sl = slice(p * PAIR, (p + 1) * PAIR)
            bp, kp, vp = b_t[:, sl], k_t[:, sl], v[:, sl]
            chains.append(dict(
                d=d, p=p, sl=sl, o_ref=o_ref, strict=strict, incl=incl, vp=vp,
                ar=jnp.concatenate([a_t[:, sl], r_t[:, sl]], axis=0),
                bk=jnp.concatenate([jnp.where(head0, bp, 0.0), jnp.where(head0, 0.0, bp),
                                    jnp.where(head0, kp, 0.0), jnp.where(head0, 0.0, kp)],
                                   axis=0).astype(BF16),
                v_bd=jnp.concatenate([jnp.where(head0, vp, 0), jnp.where(head0, 0, vp)], axis=0),
                bk_h=jnp.concatenate([b_h[:, sl], k_h[:, sl]], axis=0),
                g_last=g_last[:, sl]))

    for ch in chains:
        ch['s0'] = state_ref[ch['d'], ch['p']]
        ch['a_all'] = _dot_nt(ch['ar'], ch['bk'])
        ch['ah'] = _dot_nt(ch['ar'], ch['s0'].astype(BF16))
    for ch in chains:
        a_all, strict, incl = ch['a_all'], ch['strict'], ch['incl']
        ch['pk'] = jnp.where(strict, a_all[:L, :PAIR], 0.0)
        a_ak = jnp.where(strict, a_all[:L, PAIR:], 0.0).astype(BF16)
        ch['a_rbk'] = jnp.concatenate([jnp.where(incl, a_all[L:, :PAIR], 0.0),
                                       jnp.where(incl, a_all[L:, PAIR:], 0.0)], axis=1).astype(BF16)
        ch['w'] = ch['ah'][:L] + _dot(a_ak, ch['v_bd'])
    steps = int(math.log2(L))
    for step in range(steps):
        for ch in chains:
            pkb = ch['pk'].astype(BF16)
            wb = ch['w'].astype(BF16)
            if step < steps - 1:
                rhs = jnp.concatenate(
                    [jnp.concatenate([jnp.where(head0, pkb, 0), jnp.where(head0, wb, 0)], axis=1),
                     jnp.concatenate([jnp.where(head0, 0, pkb), jnp.where(head0, 0, wb)], axis=1)], axis=0)
                out = _dot(pkb, rhs)
                ch['pk'] = out[:, :PAIR]
                ch['w'] = ch['w'] + out[:, PAIR:]
            else:
                rhs = jnp.concatenate([jnp.where(head0, wb, 0), jnp.where(head0, 0, wb)], axis=0)
                ch['w'] = ch['w'] + _dot(pkb, rhs)
    for ch in chains:
        ub = ch['w'].astype(BF16)
        uv_bd = jnp.concatenate([jnp.where(head0, ub, 0), jnp.where(head0, 0, ub), ch['v_bd']], axis=0)
        ch['o_ref'][:, ch['sl']] = ch['ah'][L:] + _dot(ch['a_rbk'], uv_bd)
        upd = _dot_tn(jnp.concatenate([ub, ch['vp']], axis=0), ch['bk_h'])
        state_ref[ch['d'], ch['p']] = ch['s0'] * ch['g_last'] + jnp.where(bd_mask, upd, 0.0)


def _wkv(r, k, v, kk, a, lw, k_a, tri):
    B, Tp, D = r.shape
    n = Tp // CHUNK
    fwd = lambda col: pl.BlockSpec((None, CHUNK, D), lambda b, c: (b, c, col))
    bwd = lambda col: pl.BlockSpec((None, CHUNK, D), lambda b, c: (b, n - 1 - c, col))
    in_specs = ([fwd(0)] * 6 + [bwd(0)] * 4 + [bwd(1), bwd(1)]
                + [_const_spec(k_a.shape), _const_spec(tri.shape)])
    return pl.pallas_call(
        _wkv_kernel,
        grid=(B, n),
        in_specs=in_specs,
        out_specs=[fwd(0), bwd(0)],
        out_shape=[jax.ShapeDtypeStruct((B, Tp, D), F32)] * 2,
        scratch_shapes=[pltpu.VMEM((2, N_PAIRS, PAIR, PAIR), F32)],
        compiler_params=pltpu.CompilerParams(
            dimension_semantics=("parallel", "arbitrary"), vmem_limit_bytes=VMEM_LIMIT_BYTES),
        name="wkv",
    )(r, k, v, kk, a, lw, r, k, v, kk, a, lw, k_a, tri)


def _rwkv_out_kernel(of_ref, ob_ref, r_ref, k_ref, v_ref, g_ref, af_ref, ab_ref, h_ref,
                     rk_ref, ka_ref, gnw_ref, gnb_ref, wo_ref, e_ref, et_ref, out_ref):
    o = of_ref[...] + ob_ref[...]
    inv_n = 1.0 / HEAD_SIZE
    mean = _head_sum_bcast(o, e_ref, et_ref) * inv_n
    dlt = o - mean
    var = _head_sum_bcast(dlt * dlt, e_ref, et_ref) * inv_n
    on = dlt * lax.rsqrt(var + GN_EPS) * gnw_ref[...] + gnb_ref[...]
    r = r_ref[...].astype(F32)
    k = k_ref[...].astype(F32)
    v = v_ref[...].astype(F32)
    a_sum = af_ref[...].astype(F32) + ab_ref[...].astype(F32)
    rk = r * rk_ref[...] * k * (2.0 + (a_sum - 2.0) * ka_ref[...])
    on = on + _head_sum_bcast(rk, e_ref, et_ref) * v
    y = (on * g_ref[...].astype(F32)).astype(BF16)
    out_ref[...] = h_ref[...] + _dot(y, wo_ref[...])


def _rwkv_out(o_f, o_b, r, k, v, g, a, h, p, tb):
    B, Tp, D = h.shape
    row = lambda col=0: pl.BlockSpec((None, tb, D), lambda b, i: (b, i, col))
    consts = [p['r_k'], p['k_a'], p['gn_w'], p['gn_b'], p['w_o'], p['E'], p['ET']]
    return pl.pallas_call(
        _rwkv_out_kernel,
        grid=(B, Tp // tb),
        in_specs=[row()] * 6 + [row(0), row(1), row()] + [_const_spec(c.shape) for c in consts],
        out_specs=row(),
        out_shape=jax.ShapeDtypeStruct((B, Tp, D), F32),
        compiler_params=pltpu.CompilerParams(
            dimension_semantics=("parallel", "parallel"), vmem_limit_bytes=VMEM_LIMIT_BYTES),
        name="rwkv_out",
    )(o_f, o_b, r, k, v, g, a, a, h, *consts)


def _ffn_kernel(h_ref, hp_ref, hn_ref, gain_ref, win_ref, cw_ref, cb_ref, wout_ref, gfin_ref, out_ref,
                *, tb, tr, shift, final_norm):
    i = pl.program_id(1)
    n_win = tb + 2 * HALO
    if shift == 0:
        hw = jnp.concatenate([hp_ref[...], h_ref[...], hn_ref[0:HALO, :]], axis=0)
        h_res = h_ref[...]
    else:
        hw = jnp.concatenate([h_ref[shift - HALO:, :], hn_ref[0:shift + HALO, :]], axis=0)
        h_res = jnp.concatenate([h_ref[shift:, :], hn_ref[0:shift, :]], axis=0)
    rows = i * tb + (shift - HALO) + lax.broadcasted_iota(jnp.int32, (n_win, 1), 0)
    valid = jnp.logical_and(rows >= 0, rows < tr)
    xw = jnp.where(valid, _rmsnorm(hw, gain_ref[...]), 0.0).astype(BF16)
    x_mid = xw[HALO:HALO + tb]

    def in_dots(c0, width):
        u_act = _dot(xw, win_ref[:, c0:c0 + width])
        u_lin = _dot(x_mid, win_ref[:, D_FF + c0:D_FF + c0 + width])
        return u_act, u_lin

    chunks = [(c0, min(FF_CHUNK, D_FF - c0)) for c0 in range(0, D_FF, FF_CHUNK)]
    acc = None
    nxt = in_dots(*chunks[0])
    for ci, (c0, width) in enumerate(chunks):
        u_act, u_lin = nxt
        if ci + 1 < len(chunks):
            nxt = in_dots(*chunks[ci + 1])
        cw = cw_ref[:, c0:c0 + width]
        c = (pltpu.roll(u_act, 1, 0)[HALO:HALO + tb] * cw[0:1, :]
             + u_act[HALO:HALO + tb] * cw[1:2, :]
             + pltpu.roll(u_act, n_win - 1, 0)[HALO:HALO + tb] * cw[2:3, :]
             + cb_ref[:, c0:c0 + width])
        y = (c * jax.nn.sigmoid(c) * u_lin).astype(BF16)
        part = _dot(y, wout_ref[c0:c0 + width, :])
        acc = part if acc is None else acc + part
    res = h_res + acc
    if final_norm:
        res = _rmsnorm(res, gfin_ref[...])
    out_ref[...] = res


def _ffn(h, gain, w_in, conv_w, conv_b, w_out, g_final, tb, tr, out_rows, final_norm):
    B, Tp, D = h.shape
    shift = 0 if out_rows == Tp else META_SHIFT
    nblk = pl.cdiv(out_rows, tb)
    per_prev, per_next, n_next = tb // HALO, tb // NEXT_ROWS, Tp // NEXT_ROWS
    resident = lambda shape: pl.BlockSpec(shape, lambda *_: (0,) * len(shape), pipeline_mode=pl.Buffered(1))
    in_specs = [
        pl.BlockSpec((None, tb, D), lambda b, i: (b, i, 0)),
        pl.BlockSpec((None, HALO, D), lambda b, i: (b, jnp.maximum(i * per_prev - 1, 0), 0)),
        pl.BlockSpec((None, NEXT_ROWS, D), lambda b, i: (b, jnp.minimum((i + 1) * per_next, n_next - 1), 0)),
        _const_spec(gain.shape), resident(w_in.shape), _const_spec(conv_w.shape), _const_spec(conv_b.shape),
        resident(w_out.shape), _const_spec(g_final.shape),
    ]
    return pl.pallas_call(
        functools.partial(_ffn_kernel, tb=tb, tr=tr, shift=shift, final_norm=final_norm),
        grid=(B, nblk),
        in_specs=in_specs,
        out_specs=pl.BlockSpec((None, tb, D), lambda b, i: (b, i, 0)),
        out_shape=jax.ShapeDtypeStruct((B, out_rows, D), F32),
        compiler_params=pltpu.CompilerParams(
            dimension_semantics=("parallel", "parallel"), vmem_limit_bytes=VMEM_LIMIT_BYTES),
        name="ffn_final" if final_norm else "ffn",
    )(h, h, h, gain, w_in, conv_w, conv_b, w_out, g_final)


def _fnet_c_kernel(h_ref, gain_ref, cs_ref, y_ref, *, tb, tr):
    i = pl.program_id(1)
    rows = i * tb + lax.broadcasted_iota(jnp.int32, (tb, 1), 0)
    x = jnp.where(rows < tr, _rmsnorm(h_ref[...], gain_ref[...]), 0.0).astype(BF16)
    cs = cs_ref[...]
    for g in range(FNET_GROUPS):
        sl = slice(g * FNET_GROUP_DIM, (g + 1) * FNET_GROUP_DIM)
        y = _dot(x[:, sl], cs)
        y_ref[0, :, sl] = y[:, :FNET_GROUP_DIM].astype(BF16)
        y_ref[1, :, sl] = y[:, FNET_GROUP_DIM:].astype(BF16)


def _fnet_c(h, gain, cs, tb, tr):
    B, Tp, D = h.shape
    return pl.pallas_call(
        functools.partial(_fnet_c_kernel, tb=tb, tr=tr),
        grid=(B, Tp // tb),
        in_specs=[pl.BlockSpec((None, tb, D), lambda b, i: (b, i, 0)),
                  _const_spec(gain.shape), _const_spec(cs.shape)],
        out_specs=pl.BlockSpec((None, 2, tb, D), lambda b, i: (b, 0, i, 0)),
        out_shape=jax.ShapeDtypeStruct((B, 2, Tp, D), BF16),
        compiler_params=pltpu.CompilerParams(
            dimension_semantics=("parallel", "parallel"), vmem_limit_bytes=VMEM_LIMIT_BYTES),
        name="fnet_c",
    )(h, gain, cs)


def _fnet_t_kernel(wt_ref, y_ref, h_ref, wf_ref, out_ref, acc_ref, *, nk):
    kq = pl.program_id(2)

    @pl.when(kq == 0)
    def _():
        acc_ref[...] = jnp.zeros_like(acc_ref)

    acc_ref[...] += _dot(wt_ref[...], y_ref[...])

    @pl.when(kq == nk - 1)
    def _():
        out_ref[...] = h_ref[...] + _dot(acc_ref[...].astype(BF16), wf_ref[...])


def _fnet_t(wt, y, h, w_f, tm, tk):
    B, Tp, D = h.shape
    nk = (2 * Tp) // tk
    return pl.pallas_call(
        functools.partial(_fnet_t_kernel, nk=nk),
        grid=(B, Tp // tm, nk),
        in_specs=[pl.BlockSpec((tm, tk), lambda b, m, q: (m, q)),
                  pl.BlockSpec((None, tk, D), lambda b, m, q: (b, q, 0)),
                  pl.BlockSpec((None, tm, D), lambda b, m, q: (b, m, 0)),
                  _const_spec(w_f.shape)],
        out_specs=pl.BlockSpec((None, tm, D), lambda b, m, q: (b, m, 0)),
        out_shape=jax.ShapeDtypeStruct((B, Tp, D), F32),
        scratch_shapes=[pltpu.VMEM((tm, D), F32)],
        compiler_params=pltpu.CompilerParams(
            dimension_semantics=("parallel", "parallel", "arbitrary"), vmem_limit_bytes=VMEM_LIMIT_BYTES),
        name="fnet_t",
    )(wt, y.reshape(B, 2 * Tp, D), h, w_f)


def _dft_table(n, rows, cols):
    j = lax.broadcasted_iota(jnp.int32, (rows, cols), 0)
    k = lax.broadcasted_iota(jnp.int32, (rows, cols), 1)
    ang = ((j * k) % n).astype(F32) * (2.0 * math.pi / n)
    return jnp.cos(ang), jnp.sin(ang)


def _time_dft_matrix(tr, tp):
    nj1 = tp // CHUNK
    k1 = lax.broadcasted_iota(jnp.int32, (nj1, tp), 1)
    j1 = lax.broadcasted_iota(jnp.int32, (nj1, tp), 0)
    ang_a = ((CHUNK * j1 * k1) % tr).astype(F32) * (2.0 * math.pi / tr)
    ca, sa = jnp.cos(ang_a)[:, None, :], jnp.sin(ang_a)[:, None, :]
    cb, sb = _dft_table(tr, CHUNK, tp)
    cb, sb = cb[None], sb[None]
    j = lax.broadcasted_iota(jnp.int32, (tp, tp), 0)
    k = lax.broadcasted_iota(jnp.int32, (tp, tp), 1)
    ok = jnp.logical_and(j < tr, k < tr)
    scale = 1.0 / math.sqrt(tr)
    c = jnp.where(ok, (ca * cb - sa * sb).reshape(tp, tp) * scale, 0.0)
    s = jnp.where(ok, (sa * cb + ca * sb).reshape(tp, tp) * (-scale), 0.0)
    return jnp.concatenate([c, s], axis=1).astype(BF16)


def _largest_divisor(n, unit, cap):
    best = unit
    for m in range(unit, cap + 1, unit):
        if n % m == 0:
            best = m
    return best


BF16_ROWS = 16
LANES = 128
ROW_BLOCK_CAP = 448
DFT_FULL_K_CAP = 4608
DFT_K_BLOCK_CAP = 2048
DFT_M_BLOCK_CAP = 896
FFN_ROW_BLOCK_CAP = 896


def _tiles(tp):
    tb = _largest_divisor(tp, BF16_ROWS, ROW_BLOCK_CAP)
    full_k = 2 * tp <= DFT_FULL_K_CAP
    tk = 2 * tp if full_k else _largest_divisor(2 * tp, LANES, DFT_K_BLOCK_CAP)
    tm = _largest_divisor(tp, BF16_ROWS, ROW_BLOCK_CAP if full_k else DFT_M_BLOCK_CAP)
    tf = _largest_divisor(tp, NEXT_ROWS, FFN_ROW_BLOCK_CAP)
    return tb, tm, tk, tf


def _prepare(meta_tokens, norm_mix, norm_ffn, norm_final, rwkv_mu, rwkv_w_rkv, rwkv_w0, rwkv_w1, rwkv_w2,
             rwkv_a0, rwkv_a1, rwkv_a2, rwkv_g1, rwkv_g2, rwkv_k_k, rwkv_k_a, rwkv_r_k, rwkv_gn_w, rwkv_gn_b,
             rwkv_w_o, fnet_w_o, ffn_w_in, ffn_conv_w, ffn_conv_b, ffn_w_out):
    D = D_MODEL
    row = lambda x: x.reshape(1, -1).astype(F32)

    def lora_pair(w1, w2, w0):
        first = jnp.concatenate([w1[0], w1[1]], axis=1)
        z = jnp.zeros_like(w2[0])
        second = jnp.concatenate([jnp.concatenate([w2[0], z], axis=1),
                                  jnp.concatenate([z, w2[1]], axis=1)], axis=0)
        return first.astype(BF16), second.astype(BF16), jnp.concatenate([w0[0], w0[1]]).reshape(1, 2 * D)

    w1, w2, w0 = lora_pair(rwkv_w1[0], rwkv_w2[0], rwkv_w0[0])
    a1, a2, a0 = lora_pair(rwkv_a1[0], rwkv_a2[0], rwkv_a0[0])
    rank = rwkv_g1.shape[-1]
    g1 = jnp.pad(rwkv_g1[0], ((0, 0), (0, GATE_LORA_PAD - rank))).astype(BF16)
    g2 = jnp.pad(rwkv_g2[0], ((0, GATE_LORA_PAD - rank), (0, 0))).astype(BF16)
    head_of_lane = np.arange(D) // HEAD_SIZE
    e = (head_of_lane[:, None] == np.arange(128)[None, :]).astype(np.float32)
    cc, sc = _dft_table(FNET_GROUP_DIM, FNET_GROUP_DIM, FNET_GROUP_DIM)
    cs = (jnp.concatenate([cc, sc], axis=1) * (1.0 / math.sqrt(FNET_GROUP_DIM))).astype(BF16)
    tri_f = np.tril(np.ones((CHUNK, CHUNK), np.float32))
    return dict(
        meta=meta_tokens,
        norm_mix0=row(norm_mix[0]), norm_mix1=row(norm_mix[1]),
        norm_ffn0=row(norm_ffn[0]), norm_ffn1=row(norm_ffn[1]), norm_final=row(norm_final),
        mu=jnp.pad(rwkv_mu[0], ((0, 2), (0, 0))),
        w_rkv=rwkv_w_rkv[0].astype(BF16), w1=w1, w2=w2, w0=w0, a1=a1, a2=a2, a0=a0, g1=g1, g2=g2,
        k_k=row(rwkv_k_k[0]), k_a=row(rwkv_k_a[0]), r_k=row(rwkv_r_k[0]),
        gn_w=row(rwkv_gn_w[0]), gn_b=row(rwkv_gn_b[0]),
        w_o=rwkv_w_o[0].astype(BF16), w_f=fnet_w_o[0].astype(BF16),
        ffn_w_in=ffn_w_in.astype(BF16), ffn_conv_w=ffn_conv_w, ffn_conv_b=ffn_conv_b.reshape(-1, 1, D_FF),
        ffn_w_out=ffn_w_out.astype(BF16),
        E=jnp.asarray(e, BF16), ET=jnp.asarray(e.T, BF16), cs=cs,
        tri=jnp.asarray(np.stack([tri_f, tri_f.T]), BF16),
    )


def _trunk(x, p):
    B, T, D = x.shape
    tr = T + N_META
    tp = -(-tr // CHUNK) * CHUNK
    tb, tm, tk, tf = _tiles(tp)
    meta = jnp.broadcast_to(p['meta'].astype(x.dtype)[None], (B, N_META, D))
    h = jnp.concatenate([meta, x, jnp.zeros((B, tp - tr, D), x.dtype)], axis=1)

    r, k, v, kk, g, a, lw = _rwkv_proj(h, p, tb, tr)
    o_f, o_b = _wkv(r, k, v, kk, a, lw, p['k_a'], p['tri'])
    h = _rwkv_out(o_f, o_b, r, k, v, g, a, h, p, tb)
    h = _ffn(h, p['norm_ffn0'], p['ffn_w_in'][0], p['ffn_conv_w'][0], p['ffn_conv_b'][0], p['ffn_w_out'][0],
             p['norm_final'], tf, tr, out_rows=tp, final_norm=False)

    y = _fnet_c(h, p['norm_mix1'], p['cs'], tb, tr)
    h = _fnet_t(_time_dft_matrix(tr, tp), y, h, p['w_f'], tm, tk)
    h = _ffn(h, p['norm_ffn1'], p['ffn_w_in'][1], p['ffn_conv_w'][1], p['ffn_conv_b'][1], p['ffn_w_out'][1],
             p['norm_final'], tf, tr, out_rows=T, final_norm=True)
    return h


def kernel(x_prompt, x_sample, meta_tokens, norm_mix, norm_ffn, norm_final, rwkv_mu, rwkv_w_rkv, rwkv_w0, rwkv_w1, rwkv_w2, rwkv_a0, rwkv_a1, rwkv_a2, rwkv_g1, rwkv_g2, rwkv_k_k, rwkv_k_a, rwkv_r_k, rwkv_gn_w, rwkv_gn_b, rwkv_w_o, fnet_w_o, ffn_w_in, ffn_conv_w, ffn_conv_b, ffn_w_out):
    p = _prepare(meta_tokens, norm_mix, norm_ffn, norm_final, rwkv_mu, rwkv_w_rkv, rwkv_w0, rwkv_w1, rwkv_w2,
                 rwkv_a0, rwkv_a1, rwkv_a2, rwkv_g1, rwkv_g2, rwkv_k_k, rwkv_k_a, rwkv_r_k, rwkv_gn_w,
                 rwkv_gn_b, rwkv_w_o, fnet_w_o, ffn_w_in, ffn_conv_w, ffn_conv_b, ffn_w_out)
    return (_trunk(x_prompt, p), _trunk(x_sample, p))
```

```python
import functools
import math

import jax
import jax.numpy as jnp
import numpy as np
from jax import lax
from jax.experimental import pallas as pl
from jax.experimental.pallas import tpu as pltpu

D_MODEL = 1024
N_META = 16
HEAD_SIZE = 64
N_HEADS = D_MODEL // HEAD_SIZE
GATE_LORA_PAD = 256
LORA2 = 128
D_FF = 2816
FNET_GROUPS = 8
FNET_GROUP_DIM = D_MODEL // FNET_GROUPS
RMS_EPS = 1e-6
GN_EPS = 64e-5

CHUNK = 64
PAIR = 2 * HEAD_SIZE
N_PAIRS = N_HEADS // 2
HALO = 8
FF_CHUNK = 512
META_SHIFT = N_META
NEXT_ROWS = 32
VMEM_LIMIT_BYTES = 56 * 1024 * 1024

BF16 = jnp.bfloat16
F32 = jnp.float32


def _dot(a, b):
    return jnp.dot(a, b, preferred_element_type=F32)


def _dot_nt(a, b):
    return lax.dot_general(a, b, (((1,), (1,)), ((), ())), preferred_element_type=F32)


def _dot_tn(a, b):
    return lax.dot_general(a, b, (((0,), (0,)), ((), ())), preferred_element_type=F32)


def _rmsnorm(x, gain):
    ms = jnp.mean(x * x, axis=-1, keepdims=True)
    return x * lax.rsqrt(ms + RMS_EPS) * gain


def _split_bf16(x):
    hi = x.astype(BF16)
    lo = (x - hi.astype(F32)).astype(BF16)
    return hi, lo


def _head_sum(x, e_ref):
    return _dot(x.astype(BF16), e_ref[...])


def _head_bcast(s, et2_ref):
    hi, lo = _split_bf16(s)
    return _dot(jnp.concatenate([hi, lo], axis=1), et2_ref[...])


def _normed_window(h_ref, hp_ref, hn_ref, gain, i, nblk, tb, tr):
    row0 = i * tb
    rows = row0 + lax.broadcasted_iota(jnp.int32, (tb, 1), 0)
    valid = rows < tr
    x = jnp.where(valid, _rmsnorm(h_ref[...], gain), 0.0)
    xp = _rmsnorm(hp_ref[...], gain)[HALO - 1:HALO, :]
    xp = jnp.where(i > 0, xp, 0.0)
    xn = _rmsnorm(hn_ref[...], gain)[0:1, :]
    xn = jnp.where(jnp.logical_and(i < nblk - 1, row0 + tb < tr), xn, 0.0)
    return x, xp, xn, valid


def _shift_rows(x, xp, xn, tb):
    ridx = lax.broadcasted_iota(jnp.int32, (tb, 1), 0)
    x_prev = jnp.where(ridx == 0, xp, pltpu.roll(x, 1, 0))
    x_next = jnp.where(ridx == tb - 1, xn, pltpu.roll(x, tb - 1, 0))
    return x_prev, x_next


def _proj_kernel(h_ref, hp_ref, hn_ref, gain_ref, mu_ref, wrkv_ref, w1_ref, w2_ref, w0_ref,
                 a1_ref, a2_ref, a0_ref, g1_ref, g2_ref, kk_ref, e_ref, et_ref,
                 r_out, k_out, v_out, kk_out, g_out, a_out, lw_out, *, tb, tr, nblk):
    i = pl.program_id(1)
    x, xp, xn, valid = _normed_window(h_ref, hp_ref, hn_ref, gain_ref[...], i, nblk, tb, tr)
    x_prev, x_next = _shift_rows(x, xp, xn, tb)
    xx = jnp.where(valid, 0.5 * (x_prev + x_next) - x, 0.0)
    mu = mu_ref[...]

    def mix(j):
        return (x + xx * mu[j:j + 1, :]).astype(BF16)

    k = _dot(mix(2), wrkv_ref[1])
    tw = _dot(mix(1), w1_ref[...])
    ta = _dot(mix(4), a1_ref[...])
    tg = _dot(mix(5), g1_ref[...])
    r = _dot(mix(0), wrkv_ref[0])
    kkr = k * kk_ref[...]
    ssq = _head_sum(kkr * kkr, e_ref)
    v = _dot(mix(3), wrkv_ref[2])
    w_lin = w0_ref[...] + _dot(jnp.tanh(tw).astype(BF16), w2_ref[...])
    a_lin = a0_ref[...] + _dot(ta.astype(BF16), a2_ref[...])
    gate = _dot(jax.nn.sigmoid(tg).astype(BF16), g2_ref[...])
    inv = _head_bcast(1.0 / jnp.maximum(jnp.sqrt(ssq), 1e-12), et_ref)

    k_out[...] = k.astype(BF16)
    r_out[...] = r.astype(BF16)
    v_out[...] = v.astype(BF16)
    lw_out[...] = (-math.exp(-0.5)) * jax.nn.sigmoid(w_lin)
    a_out[...] = jax.nn.sigmoid(a_lin).astype(BF16)
    g_out[...] = gate.astype(BF16)
    kk_out[...] = (kkr * inv).astype(BF16)


def _row_block_specs(tb, nhalo_blocks):
    per = tb // HALO
    main = pl.BlockSpec((None, tb, D_MODEL), lambda b, i, *_: (b, i, 0))
    prev = pl.BlockSpec((None, HALO, D_MODEL), lambda b, i, *_: (b, jnp.maximum(i * per - 1, 0), 0))
    nxt = pl.BlockSpec((None, HALO, D_MODEL),
                       lambda b, i, *_: (b, jnp.minimum((i + 1) * per, nhalo_blocks - 1), 0))
    return main, prev, nxt


def _const_spec(shape):
    nd = len(shape)
    return pl.BlockSpec(shape, lambda *_: (0,) * nd)


def _rwkv_proj(h, p, tb, tr):
    B, Tp, D = h.shape
    nblk = Tp // tb
    main, prev, nxt = _row_block_specs(tb, Tp // HALO)
    row = lambda w: pl.BlockSpec((None, tb, w), lambda b, i: (b, i, 0))
    consts = [p['norm_mix0'], p['mu'], p['w_rkv'], p['w1'], p['w2'], p['w0'], p['a1'], p['a2'], p['a0'],
              p['g1'], p['g2'], p['k_k'], p['E'], p['ET']]
    out_shape = [jax.ShapeDtypeStruct((B, Tp, D), BF16)] * 5 + [
        jax.ShapeDtypeStruct((B, Tp, 2 * D), BF16), jax.ShapeDtypeStruct((B, Tp, 2 * D), F32)]
    return pl.pallas_call(
        functools.partial(_proj_kernel, tb=tb, tr=tr, nblk=nblk),
        grid=(B, nblk),
        in_specs=[main, prev, nxt] + [_const_spec(c.shape) for c in consts],
        out_specs=[row(D)] * 5 + [row(2 * D), row(2 * D)],
        out_shape=out_shape,
        compiler_params=pltpu.CompilerParams(
            dimension_semantics=("parallel", "arbitrary"), vmem_limit_bytes=VMEM_LIMIT_BYTES),
        name="rwkv_proj",
    )(h, h, h, *consts)


def _wkv_kernel(rf_ref, kf_ref, vf_ref, kkf_ref, af_ref, lwf_ref,
                rb_ref, kb_ref, vb_ref, kkb_ref, ab_ref, lwb_ref,
                ka_ref, tri_ref, of_ref, ob_ref, state_ref):
    c = pl.program_id(1)

    @pl.when(c == 0)
    def _():
        state_ref[...] = jnp.zeros_like(state_ref)

    L = CHUNK
    t_idx = lax.broadcasted_iota(jnp.int32, (L, PAIR), 0)
    lane = lax.broadcasted_iota(jnp.int32, (L, PAIR), 1)
    s_idx = lane % HEAD_SIZE
    head0 = lax.broadcasted_iota(jnp.int32, (1, PAIR), 1) < HEAD_SIZE
    bd_mask = (lax.broadcasted_iota(jnp.int32, (PAIR, PAIR), 0) // HEAD_SIZE
               == lax.broadcasted_iota(jnp.int32, (PAIR, PAIR), 1) // HEAD_SIZE)
    ka = ka_ref[...]

    dirs = (
        (0, rf_ref, kf_ref, vf_ref, kkf_ref, af_ref, lwf_ref, of_ref, s_idx < t_idx, s_idx <= t_idx, L - 1),
        (1, rb_ref, kb_ref, vb_ref, kkb_ref, ab_ref, lwb_ref, ob_ref, s_idx > t_idx, s_idx >= t_idx, 0),
    )
    chains = []
    for d, r_ref, k_ref, v_ref, kk_ref, a_ref, lw_ref, o_ref, strict, incl, last in dirs:
        lw = lw_ref[...]
        hi, lo = _split_bf16(lw)
        tri = tri_ref[d]
        cum = _dot(tri, hi) + _dot(tri, lo)
        cum_last = cum[last:last + 1, :]
        g_incl = jnp.exp(cum)
        g_excl = jnp.exp(cum - lw)
        g_inv = jnp.exp(-cum)
        g_tail = jnp.exp(cum_last - cum)
        g_last = jnp.exp(cum_last)

        kk = kk_ref[...].astype(F32)
        a = a_ref[...].astype(F32)
        kdir = k_ref[...].astype(F32) * (1.0 + (a - 1.0) * ka)
        b = kk * a
        a_t = (-kk * g_excl).astype(BF16)
        r_t = (r_ref[...].astype(F32) * g_incl).astype(BF16)
        b_t = b * g_inv
        k_t = kdir * g_inv
        b_h = (b * g_tail).astype(BF16)
        k_h = (kdir * g_tail).astype(BF16)
        v = v_ref[...]

        for p in range(N_PAIRS):
            sl = slice(p * PAIR, (p + 1) * PAIR)
            bp, kp, vp = b_t[:, sl], k_t[:, sl], v[:, sl]
            chains.append(dict(
                d=d, p=p, sl=sl, o_ref=o_ref, strict=strict, incl=incl, vp=vp,
                ar=jnp.concatenate([a_t[:, sl], r_t[:, sl]], axis=0),
                bk=jnp.concatenate([jnp.where(head0, bp, 0.0), jnp.where(head0, 0.0, bp),
                                    jnp.where(head0, kp, 0.0), jnp.where(head0, 0.0, kp)],
                                   axis=0).astype(BF16),
                v_bd=jnp.concatenate([jnp.where(head0, vp, 0), jnp.where(head0, 0, vp)], axis=0),
                bk_h=jnp.concatenate([b_h[:, sl], k_h[:, sl]], axis=0),
                g_last=g_last[:, sl]))

    for ch in chains:
        ch['s0'] = state_ref[ch['d'], ch['p']]
        ch['a_all'] = _dot_nt(ch['ar'], ch['bk'])
        ch['ah'] = _dot_nt(ch['ar'], ch['s0'].astype(BF16))
    for ch in chains:
        a_all, strict, incl = ch['a_all'], ch['strict'], ch['incl']
        ch['pk'] = jnp.where(strict, a_all[:L, :PAIR], 0.0)
        a_ak = jnp.where(strict, a_all[:L, PAIR:], 0.0).astype(BF16)
        ch['a_rbk'] = jnp.concatenate([jnp.where(incl, a_all[L:, :PAIR], 0.0),
                                       jnp.where(incl, a_all[L:, PAIR:], 0.0)], axis=1).astype(BF16)
        ch['w'] = ch['ah'][:L] + _dot(a_ak, ch['v_bd'])
    steps = int(math.log2(L))
    for step in range(steps):
        for ch in chains:
            pkb = ch['pk'].astype(BF16)
            wb = ch['w'].astype(BF16)
            if step < steps - 1:
                rhs = jnp.concatenate(
                    [jnp.concatenate([jnp.where(head0, pkb, 0), jnp.where(head0, wb, 0)], axis=1),
                     jnp.concatenate([jnp.where(head0, 0, pkb), jnp.where(head0, 0, wb)], axis=1)], axis=0)
                out = _dot(pkb, rhs)
                ch['pk'] = out[:, :PAIR]
                ch['w'] = ch['w'] + out[:, PAIR:]
            else:
                rhs = jnp.concatenate([jnp.where(head0, wb, 0), jnp.where(head0, 0, wb)], axis=0)
                ch['w'] = ch['w'] + _dot(pkb, rhs)
    for ch in chains:
        ub = ch['w'].astype(BF16)
        uv_bd = jnp.concatenate([jnp.where(head0, ub, 0), jnp.where(head0, 0, ub), ch['v_bd']], axis=0)
        ch['o_ref'][:, ch['sl']] = ch['ah'][L:] + _dot(ch['a_rbk'], uv_bd)
        upd = _dot_tn(jnp.concatenate([ub, ch['vp']], axis=0), ch['bk_h'])
        state_ref[ch['d'], ch['p']] = ch['s0'] * ch['g_last'] + jnp.where(bd_mask, upd, 0.0)


def _wkv(r, k, v, kk, a, lw, k_a, tri):
    B, Tp, D = r.shape
    n = Tp // CHUNK
    fwd = lambda col: pl.BlockSpec((None, CHUNK, D), lambda b, c: (b, c, col))
    bwd = lambda col: pl.BlockSpec((None, CHUNK, D), lambda b, c: (b, n - 1 - c, col))
    in_specs = ([fwd(0)] * 6 + [bwd(0)] * 4 + [bwd(1), bwd(1)]
                + [_const_spec(k_a.shape), _const_spec(tri.shape)])
    return pl.pallas_call(
        _wkv_kernel,
        grid=(B, n),
        in_specs=in_specs,
        out_specs=[fwd(0), bwd(0)],
        out_shape=[jax.ShapeDtypeStruct((B, Tp, D), F32)] * 2,
        scratch_shapes=[pltpu.VMEM((2, N_PAIRS, PAIR, PAIR), F32)],
        compiler_params=pltpu.CompilerParams(
            dimension_semantics=("parallel", "arbitrary"), vmem_limit_bytes=VMEM_LIMIT_BYTES),
        name="wkv",
    )(r, k, v, kk, a, lw, r, k, v, kk, a, lw, k_a, tri)


def _rwkv_out_kernel(of_ref, ob_ref, r_ref, k_ref, v_ref, g_ref, af_ref, ab_ref, h_ref,
                     rk_ref, ka_ref, gnw_ref, gnb_ref, wo_ref, e_ref, et_ref, out_ref, *, tb):
    inv_n = 1.0 / HEAD_SIZE
    half = tb // 2
    parts = [dict(rows=slice(q * half, (q + 1) * half)) for q in range(2)]
    for pt in parts:
        rows = pt['rows']
        pt['o'] = of_ref[rows, :] + ob_ref[rows, :]
        a_sum = af_ref[rows, :].astype(F32) + ab_ref[rows, :].astype(F32)
        rk = (r_ref[rows, :].astype(F32) * rk_ref[...] * k_ref[rows, :].astype(F32)
              * (2.0 + (a_sum - 2.0) * ka_ref[...]))
        pt['s_o'] = _head_sum(pt['o'], e_ref)
        pt['s_rk'] = _head_sum(rk, e_ref)
    for pt in parts:
        pt['dlt'] = pt['o'] - _head_bcast(pt['s_o'] * inv_n, et_ref)
        pt['bonus'] = _head_bcast(pt['s_rk'], et_ref)
    for pt in parts:
        pt['s_v'] = _head_sum(pt['dlt'] * pt['dlt'], e_ref)
    for pt in parts:
        rows = pt['rows']
        var = _head_bcast(pt['s_v'] * inv_n, et_ref)
        on = pt['dlt'] * lax.rsqrt(var + GN_EPS) * gnw_ref[...] + gnb_ref[...]
        on = on + pt['bonus'] * v_ref[rows, :].astype(F32)
        y = (on * g_ref[rows, :].astype(F32)).astype(BF16)
        out_ref[rows, :] = h_ref[rows, :] + _dot(y, wo_ref[...])


def _rwkv_out(o_f, o_b, r, k, v, g, a, h, p, tb):
    B, Tp, D = h.shape
    row = lambda col=0: pl.BlockSpec((None, tb, D), lambda b, i: (b, i, col))
    consts = [p['r_k'], p['k_a'], p['gn_w'], p['gn_b'], p['w_o'], p['E'], p['ET']]
    return pl.pallas_call(
        functools.partial(_rwkv_out_kernel, tb=tb),
        grid=(B, Tp // tb),
        in_specs=[row()] * 6 + [row(0), row(1), row()] + [_const_spec(c.shape) for c in consts],
        out_specs=row(),
        out_shape=jax.ShapeDtypeStruct((B, Tp, D), F32),
        compiler_params=pltpu.CompilerParams(
            dimension_semantics=("parallel", "parallel"), vmem_limit_bytes=VMEM_LIMIT_BYTES),
        name="rwkv_out",
    )(o_f, o_b, r, k, v, g, a, a, h, *consts)


def _ffn_kernel(h_ref, hp_ref, hn_ref, gain_ref, win_ref, cw_ref, cb_ref, wout_ref, gfin_ref, out_ref,
                *, tb, tr, shift, final_norm):
    i = pl.program_id(1)
    n_win = tb + 2 * HALO
    if shift == 0:
        hw = jnp.concatenate([hp_ref[...], h_ref[...], hn_ref[0:HALO, :]], axis=0)
        h_res = h_ref[...]
    else:
        hw = jnp.concatenate([h_ref[shift - HALO:, :], hn_ref[0:shift + HALO, :]], axis=0)
        h_res = jnp.concatenate([h_ref[shift:, :], hn_ref[0:shift, :]], axis=0)
    rows = i * tb + (shift - HALO) + lax.broadcasted_iota(jnp.int32, (n_win, 1), 0)
    valid = jnp.logical_and(rows >= 0, rows < tr)
    xw = jnp.where(valid, _rmsnorm(hw, gain_ref[...]), 0.0).astype(BF16)
    x_mid = xw[HALO:HALO + tb]

    def in_dots(c0, width):
        u_act = _dot(xw, win_ref[:, c0:c0 + width])
        u_lin = _dot(x_mid, win_ref[:, D_FF + c0:D_FF + c0 + width])
        return u_act, u_lin

    chunks = [(c0, min(FF_CHUNK, D_FF - c0)) for c0 in range(0, D_FF, FF_CHUNK)]
    acc = None
    nxt = in_dots(*chunks[0])
    for ci, (c0, width) in enumerate(chunks):
        u_act, u_lin = nxt
        if ci + 1 < len(chunks):
            nxt = in_dots(*chunks[ci + 1])
        cw = cw_ref[:, c0:c0 + width]
        c = (pltpu.roll(u_act, 1, 0)[HALO:HALO + tb] * cw[0:1, :]
             + u_act[HALO:HALO + tb] * cw[1:2, :]
             + pltpu.roll(u_act, n_win - 1, 0)[HALO:HALO + tb] * cw[2:3, :]
             + cb_ref[:, c0:c0 + width])
        y = (c * jax.nn.sigmoid(c) * u_lin).astype(BF16)
        part = _dot(y, wout_ref[c0:c0 + width, :])
        acc = part if acc is None else acc + part
    res = h_res + acc
    if final_norm:
        res = _rmsnorm(res, gfin_ref[...])
    out_ref[...] = res


def _ffn(h, gain, w_in, conv_w, conv_b, w_out, g_final, tb, tr, out_rows, final_norm):
    B, Tp, D = h.shape
    shift = 0 if out_rows == Tp else META_SHIFT
    nblk = pl.cdiv(out_rows, tb)
    per_prev, per_next, n_next = tb // HALO, tb // NEXT_ROWS, Tp // NEXT_ROWS
    resident = lambda shape: pl.BlockSpec(shape, lambda *_: (0,) * len(shape), pipeline_mode=pl.Buffered(1))
    in_specs = [
        pl.BlockSpec((None, tb, D), lambda b, i: (b, i, 0)),
        pl.BlockSpec((None, HALO, D), lambda b, i: (b, jnp.maximum(i * per_prev - 1, 0), 0)),
        pl.BlockSpec((None, NEXT_ROWS, D), lambda b, i: (b, jnp.minimum((i + 1) * per_next, n_next - 1), 0)),
        _const_spec(gain.shape), resident(w_in.shape), _const_spec(conv_w.shape), _const_spec(conv_b.shape),
        resident(w_out.shape), _const_spec(g_final.shape),
    ]
    return pl.pallas_call(
        functools.partial(_ffn_kernel, tb=tb, tr=tr, shift=shift, final_norm=final_norm),
        grid=(B, nblk),
        in_specs=in_specs,
        out_specs=pl.BlockSpec((None, tb, D), lambda b, i: (b, i, 0)),
        out_shape=jax.ShapeDtypeStruct((B, out_rows, D), F32),
        compiler_params=pltpu.CompilerParams(
            dimension_semantics=("parallel", "parallel"), vmem_limit_bytes=VMEM_LIMIT_BYTES),
        name="ffn_final" if final_norm else "ffn",
    )(h, h, h, gain, w_in, conv_w, conv_b, w_out, g_final)


def _fnet_c_kernel(h_ref, gain_ref, cs_ref, y_ref, *, tb, tr):
    i = pl.program_id(1)
    rows = i * tb + lax.broadcasted_iota(jnp.int32, (tb, 1), 0)
    x = jnp.where(rows < tr, _rmsnorm(h_ref[...], gain_ref[...]), 0.0).astype(BF16)
    cs = cs_ref[...]
    for g in range(FNET_GROUPS):
        sl = slice(g * FNET_GROUP_DIM, (g + 1) * FNET_GROUP_DIM)
        y = _dot(x[:, sl], cs)
        y_ref[0, :, sl] = y[:, :FNET_GROUP_DIM].astype(BF16)
        y_ref[1, :, sl] = y[:, FNET_GROUP_DIM:].astype(BF16)


def _fnet_c(h, gain, cs, tb, tr):
    B, Tp, D = h.shape
    return pl.pallas_call(
        functools.partial(_fnet_c_kernel, tb=tb, tr=tr),
        grid=(B, Tp // tb),
        in_specs=[pl.BlockSpec((None, tb, D), lambda b, i: (b, i, 0)),
                  _const_spec(gain.shape), _const_spec(cs.shape)],
        out_specs=pl.BlockSpec((None, 2, tb, D), lambda b, i: (b, 0, i, 0)),
        out_shape=jax.ShapeDtypeStruct((B, 2, Tp, D), BF16),
        compiler_params=pltpu.CompilerParams(
            dimension_semantics=("parallel", "parallel"), vmem_limit_bytes=VMEM_LIMIT_BYTES),
        name="fnet_c",
    )(h, gain, cs)


def _fnet_t_kernel(wt_ref, y_ref, h_ref, wf_ref, out_ref, acc_ref, *, nk):
    kq = pl.program_id(2)

    @pl.when(kq == 0)
    def _():
        acc_ref[...] = jnp.zeros_like(acc_ref)

    acc_ref[...] += _dot(wt_ref[...], y_ref[...])

    @pl.when(kq == nk - 1)
    def _():
        out_ref[...] = h_ref[...] + _dot(acc_ref[...].astype(BF16), wf_ref[...])


def _fnet_t(wt, y, h, w_f, tm, tk):
    B, Tp, D = h.shape
    nk = (2 * Tp) // tk
    return pl.pallas_call(
        functools.partial(_fnet_t_kernel, nk=nk),
        grid=(B, Tp // tm, nk),
        in_specs=[pl.BlockSpec((tm, tk), lambda b, m, q: (m, q)),
                  pl.BlockSpec((None, tk, D), lambda b, m, q: (b, q, 0)),
                  pl.BlockSpec((None, tm, D), lambda b, m, q: (b, m, 0)),
                  _const_spec(w_f.shape)],
        out_specs=pl.BlockSpec((None, tm, D), lambda b, m, q: (b, m, 0)),
        out_shape=jax.ShapeDtypeStruct((B, Tp, D), F32),
        scratch_shapes=[pltpu.VMEM((tm, D), F32)],
        compiler_params=pltpu.CompilerParams(
            dimension_semantics=("parallel", "parallel", "arbitrary"), vmem_limit_bytes=VMEM_LIMIT_BYTES),
        name="fnet_t",
    )(wt, y.reshape(B, 2 * Tp, D), h, w_f)


def _dft_table(n, rows, cols):
    j = lax.broadcasted_iota(jnp.int32, (rows, cols), 0)
    k = lax.broadcasted_iota(jnp.int32, (rows, cols), 1)
    ang = ((j * k) % n).astype(F32) * (2.0 * math.pi / n)
    return jnp.cos(ang), jnp.sin(ang)


def _time_dft_matrix(tr, tp):
    nj1 = tp // CHUNK
    k1 = lax.broadcasted_iota(jnp.int32, (nj1, tp), 1)
    j1 = lax.broadcasted_iota(jnp.int32, (nj1, tp), 0)
    ang_a = ((CHUNK * j1 * k1) % tr).astype(F32) * (2.0 * math.pi / tr)
    ca, sa = jnp.cos(ang_a)[:, None, :], jnp.sin(ang_a)[:, None, :]
    cb, sb = _dft_table(tr, CHUNK, tp)
    cb, sb = cb[None], sb[None]
    j = lax.broadcasted_iota(jnp.int32, (tp, tp), 0)
    k = lax.broadcasted_iota(jnp.int32, (tp, tp), 1)
    ok = jnp.logical_and(j < tr, k < tr)
    scale = 1.0 / math.sqrt(tr)
    c = jnp.where(ok, (ca * cb - sa * sb).reshape(tp, tp) * scale, 0.0)
    s = jnp.where(ok, (sa * cb + ca * sb).reshape(tp, tp) * (-scale), 0.0)
    return jnp.concatenate([c, s], axis=1).astype(BF16)


def _largest_divisor(n, unit, cap):
    best = unit
    for m in range(unit, cap + 1, unit):
        if n % m == 0:
            best = m
    return best


BF16_ROWS = 16
LANES = 128
ROW_BLOCK_CAP = 448
DFT_FULL_K_CAP = 4608
DFT_K_BLOCK_CAP = 2048
DFT_M_BLOCK_CAP = 896
FFN_ROW_BLOCK_CAP = 896


def _tiles(tp):
    tb = _largest_divisor(tp, BF16_ROWS, ROW_BLOCK_CAP)
    full_k = 2 * tp <= DFT_FULL_K_CAP
    tk = 2 * tp if full_k else _largest_divisor(2 * tp, LANES, DFT_K_BLOCK_CAP)
    tm = _largest_divisor(tp, BF16_ROWS, ROW_BLOCK_CAP if full_k else DFT_M_BLOCK_CAP)
    tf = _largest_divisor(tp, NEXT_ROWS, FFN_ROW_BLOCK_CAP)
    return tb, tm, tk, tf


def _prepare(meta_tokens, norm_mix, norm_ffn, norm_final, rwkv_mu, rwkv_w_rkv, rwkv_w0, rwkv_w1, rwkv_w2,
             rwkv_a0, rwkv_a1, rwkv_a2, rwkv_g1, rwkv_g2, rwkv_k_k, rwkv_k_a, rwkv_r_k, rwkv_gn_w, rwkv_gn_b,
             rwkv_w_o, fnet_w_o, ffn_w_in, ffn_conv_w, ffn_conv_b, ffn_w_out):
    D = D_MODEL
    row = lambda x: x.reshape(1, -1).astype(F32)

    def lora_pair(w1, w2, w0):
        first = jnp.concatenate([w1[0], w1[1]], axis=1)
        z = jnp.zeros_like(w2[0])
        second = jnp.concatenate([jnp.concatenate([w2[0], z], axis=1),
                                  jnp.concatenate([z, w2[1]], axis=1)], axis=0)
        return first.astype(BF16), second.astype(BF16), jnp.concatenate([w0[0], w0[1]]).reshape(1, 2 * D)

    w1, w2, w0 = lora_pair(rwkv_w1[0], rwkv_w2[0], rwkv_w0[0])
    a1, a2, a0 = lora_pair(rwkv_a1[0], rwkv_a2[0], rwkv_a0[0])
    rank = rwkv_g1.shape[-1]
    g1 = jnp.pad(rwkv_g1[0], ((0, 0), (0, GATE_LORA_PAD - rank))).astype(BF16)
    g2 = jnp.pad(rwkv_g2[0], ((0, GATE_LORA_PAD - rank), (0, 0))).astype(BF16)
    head_of_lane = np.arange(D) // HEAD_SIZE
    e = (head_of_lane[:, None] == np.arange(128)[None, :]).astype(np.float32)
    cc, sc = _dft_table(FNET_GROUP_DIM, FNET_GROUP_DIM, FNET_GROUP_DIM)
    cs = (jnp.concatenate([cc, sc], axis=1) * (1.0 / math.sqrt(FNET_GROUP_DIM))).astype(BF16)
    tri_f = np.tril(np.ones((CHUNK, CHUNK), np.float32))
    return dict(
        meta=meta_tokens,
        norm_mix0=row(norm_mix[0]), norm_mix1=row(norm_mix[1]),
        norm_ffn0=row(norm_ffn[0]), norm_ffn1=row(norm_ffn[1]), norm_final=row(norm_final),
        mu=jnp.pad(rwkv_mu[0], ((0, 2), (0, 0))),
        w_rkv=rwkv_w_rkv[0].astype(BF16), w1=w1, w2=w2, w0=w0, a1=a1, a2=a2, a0=a0, g1=g1, g2=g2,
        k_k=row(rwkv_k_k[0]), k_a=row(rwkv_k_a[0]), r_k=row(rwkv_r_k[0]),
        gn_w=row(rwkv_gn_w[0]), gn_b=row(rwkv_gn_b[0]),
        w_o=rwkv_w_o[0].astype(BF16), w_f=fnet_w_o[0].astype(BF16),
        ffn_w_in=ffn_w_in.astype(BF16), ffn_conv_w=ffn_conv_w, ffn_conv_b=ffn_conv_b.reshape(-1, 1, D_FF),
        ffn_w_out=ffn_w_out.astype(BF16),
        E=jnp.asarray(e, BF16), ET=jnp.asarray(np.concatenate([e.T, e.T]), BF16), cs=cs,
        tri=jnp.asarray(np.stack([tri_f, tri_f.T]), BF16),
    )


def _trunk(x, p):
    B, T, D = x.shape
    tr = T + N_META
    tp = -(-tr // CHUNK) * CHUNK
    tb, tm, tk, tf = _tiles(tp)
    meta = jnp.broadcast_to(p['meta'].astype(x.dtype)[None], (B, N_META, D))
    h = jnp.concatenate([meta, x, jnp.zeros((B, tp - tr, D), x.dtype)], axis=1)

    r, k, v, kk, g, a, lw = _rwkv_proj(h, p, tb, tr)
    o_f, o_b = _wkv(r, k, v, kk, a, lw, p['k_a'], p['tri'])
    h = _rwkv_out(o_f, o_b, r, k, v, g, a, h, p, tb)
    h = _ffn(h, p['norm_ffn0'], p['ffn_w_in'][0], p['ffn_conv_w'][0], p['ffn_conv_b'][0], p['ffn_w_out'][0],
             p['norm_final'], tf, tr, out_rows=tp, final_norm=False)

    y = _fnet_c(h, p['norm_mix1'], p['cs'], tb, tr)
    h = _fnet_t(_time_dft_matrix(tr, tp), y, h, p['w_f'], tm, tk)
    h = _ffn(h, p['norm_ffn1'], p['ffn_w_in'][1], p['ffn_conv_w'][1], p['ffn_conv_b'][1], p['ffn_w_out'][1],
             p['norm_final'], tf, tr, out_rows=T, final_norm=True)
    return h


def kernel(x_prompt, x_sample, meta_tokens, norm_mix, norm_ffn, norm_final, rwkv_mu, rwkv_w_rkv, rwkv_w0, rwkv_w1, rwkv_w2, rwkv_a0, rwkv_a1, rwkv_a2, rwkv_g1, rwkv_g2, rwkv_k_k, rwkv_k_a, rwkv_r_k, rwkv_gn_w, rwkv_gn_b, rwkv_w_o, fnet_w_o, ffn_w_in, ffn_conv_w, ffn_conv_b, ffn_w_out):
    p = _prepare(meta_tokens, norm_mix, norm_ffn, norm_final, rwkv_mu, rwkv_w_rkv, rwkv_w0, rwkv_w1, rwkv_w2,
                 rwkv_a0, rwkv_a1, rwkv_a2, rwkv_g1, rwkv_g2, rwkv_k_k, rwkv_k_a, rwkv_r_k, rwkv_gn_w,
                 rwkv_gn_b, rwkv_w_o, fnet_w_o, ffn_w_in, ffn_conv_w, ffn_conv_b, ffn_w_out)
    return (_trunk(x_prompt, p), _trunk(x_sample, p))
```

```python
import functools
import math

import jax
import jax.numpy as jnp
import numpy as np
from jax import lax
from jax.experimental import pallas as pl
from jax.experimental.pallas import tpu as pltpu

D_MODEL = 1024
N_META = 16
HEAD_SIZE = 64
N_HEADS = D_MODEL // HEAD_SIZE
GATE_LORA_PAD = 256
LORA2 = 128
D_FF = 2816
FNET_GROUPS = 8
FNET_GROUP_DIM = D_MODEL // FNET_GROUPS
RMS_EPS = 1e-6
GN_EPS = 64e-5

CHUNK = 64
PAIR = 2 * HEAD_SIZE
N_PAIRS = N_HEADS // 2
HALO = 8
FF_CHUNK = 256
META_SHIFT = N_META
NEXT_ROWS = 32
WKV_SUBCHUNK_CAP = 5
VMEM_LIMIT_BYTES = 56 * 1024 * 1024

BF16 = jnp.bfloat16
F32 = jnp.float32


def _dot(a, b):
    return jnp.dot(a, b, preferred_element_type=F32)


def _dot_nt(a, b):
    return lax.dot_general(a, b, (((1,), (1,)), ((), ())), preferred_element_type=F32)


def _dot_tn(a, b):
    return lax.dot_general(a, b, (((0,), (0,)), ((), ())), preferred_element_type=F32)


def _rmsnorm(x, gain):
    ms = jnp.mean(x * x, axis=-1, keepdims=True)
    return x * lax.rsqrt(ms + RMS_EPS) * gain


def _split_bf16(x):
    hi = x.astype(BF16)
    lo = (x - hi.astype(F32)).astype(BF16)
    return hi, lo


def _head_sum(x, e_ref):
    return _dot(x.astype(BF16), e_ref[...])


def _head_bcast(s, et2_ref):
    hi, lo = _split_bf16(s)
    return _dot(jnp.concatenate([hi, lo], axis=1), et2_ref[...])


def _normed_window(h_ref, hp_ref, hn_ref, gain, i, nblk, tb, tr):
    row0 = i * tb
    rows = row0 + lax.broadcasted_iota(jnp.int32, (tb, 1), 0)
    valid = rows < tr
    x = jnp.where(valid, _rmsnorm(h_ref[...], gain), 0.0)
    xp = _rmsnorm(hp_ref[...], gain)[HALO - 1:HALO, :]
    xp = jnp.where(i > 0, xp, 0.0)
    xn = _rmsnorm(hn_ref[...], gain)[0:1, :]
    xn = jnp.where(jnp.logical_and(i < nblk - 1, row0 + tb < tr), xn, 0.0)
    return x, xp, xn, valid


def _shift_rows(x, xp, xn, tb):
    ridx = lax.broadcasted_iota(jnp.int32, (tb, 1), 0)
    x_prev = jnp.where(ridx == 0, xp, pltpu.roll(x, 1, 0))
    x_next = jnp.where(ridx == tb - 1, xn, pltpu.roll(x, tb - 1, 0))
    return x_prev, x_next


def _proj_kernel(h_ref, hp_ref, hn_ref, gain_ref, mu_ref, wrkv_ref, w1_ref, w2_ref, w0_ref,
                 a1_ref, a2_ref, a0_ref, g1_ref, g2_ref, kk_ref, e_ref, et_ref,
                 r_out, k_out, v_out, kk_out, g_out, a_out, lw_out, *, tb, tr, nblk):
    i = pl.program_id(1)
    x, xp, xn, valid = _normed_window(h_ref, hp_ref, hn_ref, gain_ref[...], i, nblk, tb, tr)
    x_prev, x_next = _shift_rows(x, xp, xn, tb)
    xx = jnp.where(valid, 0.5 * (x_prev + x_next) - x, 0.0)
    mu = mu_ref[...]

    def mix(j):
        return (x + xx * mu[j:j + 1, :]).astype(BF16)

    k = _dot(mix(2), wrkv_ref[1])
    tw = _dot(mix(1), w1_ref[...])
    ta = _dot(mix(4), a1_ref[...])
    tg = _dot(mix(5), g1_ref[...])
    r = _dot(mix(0), wrkv_ref[0])
    kkr = k * kk_ref[...]
    ssq = _head_sum(kkr * kkr, e_ref)
    v = _dot(mix(3), wrkv_ref[2])
    w_lin = w0_ref[...] + _dot(jnp.tanh(tw).astype(BF16), w2_ref[...])
    a_lin = a0_ref[...] + _dot(ta.astype(BF16), a2_ref[...])
    gate = _dot(jax.nn.sigmoid(tg).astype(BF16), g2_ref[...])
    inv = _head_bcast(1.0 / jnp.maximum(jnp.sqrt(ssq), 1e-12), et_ref)

    k_out[...] = k.astype(BF16)
    r_out[...] = r.astype(BF16)
    v_out[...] = v.astype(BF16)
    lw_out[...] = (-math.exp(-0.5)) * jax.nn.sigmoid(w_lin)
    a_out[...] = jax.nn.sigmoid(a_lin).astype(BF16)
    g_out[...] = gate.astype(BF16)
    kk_out[...] = (kkr * inv).astype(BF16)


def _row_block_specs(tb, nhalo_blocks):
    per = tb // HALO
    main = pl.BlockSpec((None, tb, D_MODEL), lambda b, i, *_: (b, i, 0))
    prev = pl.BlockSpec((None, HALO, D_MODEL), lambda b, i, *_: (b, jnp.maximum(i * per - 1, 0), 0))
    nxt = pl.BlockSpec((None, HALO, D_MODEL),
                       lambda b, i, *_: (b, jnp.minimum((i + 1) * per, nhalo_blocks - 1), 0))
    return main, prev, nxt


def _const_spec(shape):
    nd = len(shape)
    return pl.BlockSpec(shape, lambda *_: (0,) * nd)


def _rwkv_proj(h, p, tb, tr):
    B, Tp, D = h.shape
    nblk = Tp // tb
    main, prev, nxt = _row_block_specs(tb, Tp // HALO)
    row = lambda w: pl.BlockSpec((None, tb, w), lambda b, i: (b, i, 0))
    consts = [p['norm_mix0'], p['mu'], p['w_rkv'], p['w1'], p['w2'], p['w0'], p['a1'], p['a2'], p['a0'],
              p['g1'], p['g2'], p['k_k'], p['E'], p['ET']]
    out_shape = [jax.ShapeDtypeStruct((B, Tp, D), BF16)] * 5 + [
        jax.ShapeDtypeStruct((B, Tp, 2 * D), BF16), jax.ShapeDtypeStruct((B, Tp, 2 * D), F32)]
    return pl.pallas_call(
        functools.partial(_proj_kernel, tb=tb, tr=tr, nblk=nblk),
        grid=(B, nblk),
        in_specs=[main, prev, nxt] + [_const_spec(c.shape) for c in consts],
        out_specs=[row(D)] * 5 + [row(2 * D), row(2 * D)],
        out_shape=out_shape,
        compiler_params=pltpu.CompilerParams(
            dimension_semantics=("parallel", "arbitrary"), vmem_limit_bytes=VMEM_LIMIT_BYTES),
        name="rwkv_proj",
    )(h, h, h, *consts)


def _wkv_kernel(rf_ref, kf_ref, vf_ref, kkf_ref, af_ref, lwf_ref,
                rb_ref, kb_ref, vb_ref, kkb_ref, ab_ref, lwb_ref,
                ka_ref, tri_ref, of_ref, ob_ref, state_ref, *, nsub):
    c = pl.program_id(1)

    @pl.when(c == 0)
    def _():
        state_ref[...] = jnp.zeros_like(state_ref)

    def sub_chunk(sub, carry):
        _wkv_chunk(rf_ref, kf_ref, vf_ref, kkf_ref, af_ref, lwf_ref, rb_ref, kb_ref, vb_ref, kkb_ref, ab_ref,
                   lwb_ref, ka_ref, tri_ref, of_ref, ob_ref, state_ref,
                   pl.ds(pl.multiple_of(sub * CHUNK, CHUNK), CHUNK),
                   pl.ds(pl.multiple_of((nsub - 1 - sub) * CHUNK, CHUNK), CHUNK))
        return carry

    lax.fori_loop(0, nsub, sub_chunk, 0)


def _wkv_chunk(rf_ref, kf_ref, vf_ref, kkf_ref, af_ref, lwf_ref, rb_ref, kb_ref, vb_ref, kkb_ref, ab_ref, lwb_ref,
               ka_ref, tri_ref, of_ref, ob_ref, state_ref, rows_f, rows_b):
    L = CHUNK
    t_idx = lax.broadcasted_iota(jnp.int32, (L, PAIR), 0)
    lane = lax.broadcasted_iota(jnp.int32, (L, PAIR), 1)
    s_idx = lane % HEAD_SIZE
    head0 = lax.broadcasted_iota(jnp.int32, (1, PAIR), 1) < HEAD_SIZE
    bd_mask = (lax.broadcasted_iota(jnp.int32, (PAIR, PAIR), 0) // HEAD_SIZE
               == lax.broadcasted_iota(jnp.int32, (PAIR, PAIR), 1) // HEAD_SIZE)
    ka = ka_ref[...]

    dirs = (
        (0, rows_f, rf_ref, kf_ref, vf_ref, kkf_ref, af_ref, lwf_ref, of_ref, s_idx < t_idx, s_idx <= t_idx, L - 1),
        (1, rows_b, rb_ref, kb_ref, vb_ref, kkb_ref, ab_ref, lwb_ref, ob_ref, s_idx > t_idx, s_idx >= t_idx, 0),
    )
    chains = []
    for d, rows, r_ref, k_ref, v_ref, kk_ref, a_ref, lw_ref, o_ref, strict, incl, last in dirs:
        lw = lw_ref[rows, :]
        hi, lo = _split_bf16(lw)
        tri = tri_ref[d]
        cum = _dot(tri, hi) + _dot(tri, lo)
        cum_last = cum[last:last + 1, :]
        g_incl = jnp.exp(cum)
        g_excl = jnp.exp(cum - lw)
        g_inv = jnp.exp(-cum)
        g_tail = jnp.exp(cum_last - cum)
        g_last = jnp.exp(cum_last)

        kk = kk_ref[rows, :].astype(F32)
        a = a_ref[rows, :].astype(F32)
        kdir = k_ref[rows, :].astype(F32) * (1.0 + (a - 1.0) * ka)
        b = kk * a
        a_t = (-kk * g_excl).astype(BF16)
        r_t = (r_ref[rows, :].astype(F32) * g_incl).astype(BF16)
        b_t = b * g_inv
        k_t = kdir * g_inv
        b_h = (b * g_tail).astype(BF16)
        k_h = (kdir * g_tail).astype(BF16)
        v = v_ref[rows, :]

        for p in range(N_PAIRS):
            sl = slice(p * PAIR, (p + 1) * PAIR)
            bp, kp, vp = b_t[:, sl], k_t[:, sl], v[:, sl]
            chains.append(dict(
                d=d, p=p, sl=sl, rows=rows, o_ref=o_ref, strict=strict, incl=incl, vp=vp,
                ar=jnp.concatenate([a_t[:, sl], r_t[:, sl]], axis=0),
                bk=jnp.concatenate([jnp.where(head0, bp, 0.0), jnp.where(head0, 0.0, bp),
                                    jnp.where(head0, kp, 0.0), jnp.where(head0, 0.0, kp)],
                                   axis=0).astype(BF16),
                v_bd=jnp.concatenate([jnp.where(head0, vp, 0), jnp.where(head0, 0, vp)], axis=0),
                bk_h=jnp.concatenate([b_h[:, sl], k_h[:, sl]], axis=0),
                g_last=g_last[:, sl]))

    for ch in chains:
        ch['s0'] = state_ref[ch['d'], ch['p']]
        ch['a_all'] = _dot_nt(ch['ar'], ch['bk'])
        ch['ah'] = _dot_nt(ch['ar'], ch['s0'].astype(BF16))
    for ch in chains:
        a_all, strict, incl = ch['a_all'], ch['strict'], ch['incl']
        ch['pk'] = jnp.where(strict, a_all[:L, :PAIR], 0.0)
        a_ak = jnp.where(strict, a_all[:L, PAIR:], 0.0).astype(BF16)
        ch['a_rbk'] = jnp.concatenate([jnp.where(incl, a_all[L:, :PAIR], 0.0),
                                       jnp.where(incl, a_all[L:, PAIR:], 0.0)], axis=1).astype(BF16)
        ch['w'] = ch['ah'][:L] + _dot(a_ak, ch['v_bd'])
    steps = int(math.log2(L))
    for step in range(steps):
        for ch in chains:
            pkb = ch['pk'].astype(BF16)
            wb = ch['w'].astype(BF16)
            if step < steps - 1:
                rhs = jnp.concatenate(
                    [jnp.concatenate([jnp.where(head0, pkb, 0), jnp.where(head0, wb, 0)], axis=1),
                     jnp.concatenate([jnp.where(head0, 0, pkb), jnp.where(head0, 0, wb)], axis=1)], axis=0)
                out = _dot(pkb, rhs)
                ch['pk'] = out[:, :PAIR]
                ch['w'] = ch['w'] + out[:, PAIR:]
            else:
                rhs = jnp.concatenate([jnp.where(head0, wb, 0), jnp.where(head0, 0, wb)], axis=0)
                ch['w'] = ch['w'] + _dot(pkb, rhs)
    for ch in chains:
        ub = ch['w'].astype(BF16)
        uv_bd = jnp.concatenate([jnp.where(head0, ub, 0), jnp.where(head0, 0, ub), ch['v_bd']], axis=0)
        ch['o_ref'][ch['rows'], ch['sl']] = (ch['ah'][L:] + _dot(ch['a_rbk'], uv_bd)).astype(ch['o_ref'].dtype)
        upd = _dot_tn(jnp.concatenate([ub, ch['vp']], axis=0), ch['bk_h'])
        state_ref[ch['d'], ch['p']] = ch['s0'] * ch['g_last'] + jnp.where(bd_mask, upd, 0.0)


def _wkv(r, k, v, kk, a, lw, k_a, tri):
    B, Tp, D = r.shape
    nsub = _largest_divisor(Tp // CHUNK, 1, WKV_SUBCHUNK_CAP)
    rows = nsub * CHUNK
    n = Tp // rows
    fwd = lambda col: pl.BlockSpec((None, rows, D), lambda b, c: (b, c, col))
    bwd = lambda col: pl.BlockSpec((None, rows, D), lambda b, c: (b, n - 1 - c, col))
    in_specs = ([fwd(0)] * 6 + [bwd(0)] * 4 + [bwd(1), bwd(1)]
                + [_const_spec(k_a.shape), _const_spec(tri.shape)])
    return pl.pallas_call(
        functools.partial(_wkv_kernel, nsub=nsub),
        grid=(B, n),
        in_specs=in_specs,
        out_specs=[fwd(0), bwd(0)],
        out_shape=[jax.ShapeDtypeStruct((B, Tp, D), BF16)] * 2,
        scratch_shapes=[pltpu.VMEM((2, N_PAIRS, PAIR, PAIR), F32)],
        compiler_params=pltpu.CompilerParams(
            dimension_semantics=("parallel", "arbitrary"), vmem_limit_bytes=VMEM_LIMIT_BYTES),
        name="wkv",
    )(r, k, v, kk, a, lw, r, k, v, kk, a, lw, k_a, tri)


def _rwkv_out_kernel(of_ref, ob_ref, r_ref, k_ref, v_ref, g_ref, af_ref, ab_ref, h_ref,
                     rk_ref, ka_ref, gnw_ref, gnb_ref, wo_ref, e_ref, et_ref, out_ref, *, tb):
    inv_n = 1.0 / HEAD_SIZE
    half = tb // 2
    parts = [dict(rows=slice(q * half, (q + 1) * half)) for q in range(2)]
    for pt in parts:
        rows = pt['rows']
        pt['o'] = of_ref[rows, :].astype(F32) + ob_ref[rows, :].astype(F32)
        a_sum = af_ref[rows, :].astype(F32) + ab_ref[rows, :].astype(F32)
        rk = (r_ref[rows, :].astype(F32) * rk_ref[...] * k_ref[rows, :].astype(F32)
              * (2.0 + (a_sum - 2.0) * ka_ref[...]))
        pt['s_o'] = _head_sum(pt['o'], e_ref)
        pt['s_rk'] = _head_sum(rk, e_ref)
    for pt in parts:
        pt['dlt'] = pt['o'] - _head_bcast(pt['s_o'] * inv_n, et_ref)
        pt['bonus'] = _head_bcast(pt['s_rk'], et_ref)
    for pt in parts:
        pt['s_v'] = _head_sum(pt['dlt'] * pt['dlt'], e_ref)
    for pt in parts:
        rows = pt['rows']
        var = _head_bcast(pt['s_v'] * inv_n, et_ref)
        on = pt['dlt'] * lax.rsqrt(var + GN_EPS) * gnw_ref[...] + gnb_ref[...]
        on = on + pt['bonus'] * v_ref[rows, :].astype(F32)
        y = (on * g_ref[rows, :].astype(F32)).astype(BF16)
        out_ref[rows, :] = h_ref[rows, :] + _dot(y, wo_ref[...])


def _rwkv_out(o_f, o_b, r, k, v, g, a, h, p, tb):
    B, Tp, D = h.shape
    row = lambda col=0: pl.BlockSpec((None, tb, D), lambda b, i: (b, i, col))
    consts = [p['r_k'], p['k_a'], p['gn_w'], p['gn_b'], p['w_o'], p['E'], p['ET']]
    return pl.pallas_call(
        functools.partial(_rwkv_out_kernel, tb=tb),
        grid=(B, Tp // tb),
        in_specs=[row()] * 6 + [row(0), row(1), row()] + [_const_spec(c.shape) for c in consts],
        out_specs=row(),
        out_shape=jax.ShapeDtypeStruct((B, Tp, D), F32),
        compiler_params=pltpu.CompilerParams(
            dimension_semantics=("parallel", "parallel"), vmem_limit_bytes=VMEM_LIMIT_BYTES),
        name="rwkv_out",
    )(o_f, o_b, r, k, v, g, a, a, h, *consts)


def _ffn_kernel(h_ref, hp_ref, hn_ref, gain_ref, win_ref, cw_ref, cb_ref, wout_ref, gfin_ref, out_ref,
                *, tb, tr, shift, final_norm):
    i = pl.program_id(1)
    n_win = tb + 2 * HALO
    if shift == 0:
        hw = jnp.concatenate([hp_ref[...], h_ref[...], hn_ref[0:HALO, :]], axis=0)
        h_res = h_ref[...]
    else:
        hw = jnp.concatenate([h_ref[shift - HALO:, :], hn_ref[0:shift + HALO, :]], axis=0)
        h_res = jnp.concatenate([h_ref[shift:, :], hn_ref[0:shift, :]], axis=0)
    rows = i * tb + (shift - HALO) + lax.broadcasted_iota(jnp.int32, (n_win, 1), 0)
    valid = jnp.logical_and(rows >= 0, rows < tr)
    xw = jnp.where(valid, _rmsnorm(hw, gain_ref[...]), 0.0).astype(BF16)
    x_mid = xw[HALO:HALO + tb]

    def in_dots(c0, width):
        u_act = _dot(xw, win_ref[:, c0:c0 + width])
        u_lin = _dot(x_mid, win_ref[:, D_FF + c0:D_FF + c0 + width])
        return u_act, u_lin

    chunks = [(c0, min(FF_CHUNK, D_FF - c0)) for c0 in range(0, D_FF, FF_CHUNK)]
    acc = None
    nxt = in_dots(*chunks[0])
    for ci, (c0, width) in enumerate(chunks):
        u_act, u_lin = nxt
        if ci + 1 < len(chunks):
            nxt = in_dots(*chunks[ci + 1])
        cw = cw_ref[:, c0:c0 + width]
        c = (pltpu.roll(u_act, 1, 0)[HALO:HALO + tb] * cw[0:1, :]
             + u_act[HALO:HALO + tb] * cw[1:2, :]
             + pltpu.roll(u_act, n_win - 1, 0)[HALO:HALO + tb] * cw[2:3, :]
             + cb_ref[:, c0:c0 + width])
        y = (c * jax.nn.sigmoid(c) * u_lin).astype(BF16)
        part = _dot(y, wout_ref[c0:c0 + width, :])
        acc = part if acc is None else acc + part
    res = h_res + acc
    if final_norm:
        res = _rmsnorm(res, gfin_ref[...])
    out_ref[...] = res


def _ffn(h, gain, w_in, conv_w, conv_b, w_out, g_final, tb, tr, out_rows, final_norm):
    B, Tp, D = h.shape
    shift = 0 if out_rows == Tp else META_SHIFT
    nblk = pl.cdiv(out_rows, tb)
    per_prev, per_next, n_next = tb // HALO, tb // NEXT_ROWS, Tp // NEXT_ROWS
    resident = lambda shape: pl.BlockSpec(shape, lambda *_: (0,) * len(shape), pipeline_mode=pl.Buffered(1))
    in_specs = [
        pl.BlockSpec((None, tb, D), lambda b, i: (b, i, 0)),
        pl.BlockSpec((None, HALO, D), lambda b, i: (b, jnp.maximum(i * per_prev - 1, 0), 0)),
        pl.BlockSpec((None, NEXT_ROWS, D), lambda b, i: (b, jnp.minimum((i + 1) * per_next, n_next - 1), 0)),
        _const_spec(gain.shape), resident(w_in.shape), _const_spec(conv_w.shape), _const_spec(conv_b.shape),
        resident(w_out.shape), _const_spec(g_final.shape),
    ]
    return pl.pallas_call(
        functools.partial(_ffn_kernel, tb=tb, tr=tr, shift=shift, final_norm=final_norm),
        grid=(B, nblk),
        in_specs=in_specs,
        out_specs=pl.BlockSpec((None, tb, D), lambda b, i: (b, i, 0)),
        out_shape=jax.ShapeDtypeStruct((B, out_rows, D), F32),
        compiler_params=pltpu.CompilerParams(
            dimension_semantics=("parallel", "parallel"), vmem_limit_bytes=VMEM_LIMIT_BYTES),
        name="ffn_final" if final_norm else "ffn",
    )(h, h, h, gain, w_in, conv_w, conv_b, w_out, g_final)


def _fnet_c_kernel(h_ref, gain_ref, cs_ref, y_ref, *, tb, tr):
    i = pl.program_id(1)
    rows = i * tb + lax.broadcasted_iota(jnp.int32, (tb, 1), 0)
    x = jnp.where(rows < tr, _rmsnorm(h_ref[...], gain_ref[...]), 0.0).astype(BF16)
    cs = cs_ref[...]
    for g in range(FNET_GROUPS):
        sl = slice(g * FNET_GROUP_DIM, (g + 1) * FNET_GROUP_DIM)
        y = _dot(x[:, sl], cs)
        y_ref[0, :, sl] = y[:, :FNET_GROUP_DIM].astype(BF16)
        y_ref[1, :, sl] = y[:, FNET_GROUP_DIM:].astype(BF16)


def _fnet_c(h, gain, cs, tb, tr):
    B, Tp, D = h.shape
    return pl.pallas_call(
        functools.partial(_fnet_c_kernel, tb=tb, tr=tr),
        grid=(B, Tp // tb),
        in_specs=[pl.BlockSpec((None, tb, D), lambda b, i: (b, i, 0)),
                  _const_spec(gain.shape), _const_spec(cs.shape)],
        out_specs=pl.BlockSpec((None, 2, tb, D), lambda b, i: (b, 0, i, 0)),
        out_shape=jax.ShapeDtypeStruct((B, 2, Tp, D), BF16),
        compiler_params=pltpu.CompilerParams(
            dimension_semantics=("parallel", "parallel"), vmem_limit_bytes=VMEM_LIMIT_BYTES),
        name="fnet_c",
    )(h, gain, cs)


def _fnet_t_kernel(wt_ref, y_ref, h_ref, wf_ref, out_ref, acc_ref, *, nk):
    kq = pl.program_id(2)

    @pl.when(kq == 0)
    def _():
        acc_ref[...] = jnp.zeros_like(acc_ref)

    acc_ref[...] += _dot(wt_ref[...], y_ref[...])

    @pl.when(kq == nk - 1)
    def _():
        out_ref[...] = h_ref[...] + _dot(acc_ref[...].astype(BF16), wf_ref[...])


def _fnet_t(wt, y, h, w_f, tm, tk):
    B, Tp, D = h.shape
    nk = (2 * Tp) // tk
    return pl.pallas_call(
        functools.partial(_fnet_t_kernel, nk=nk),
        grid=(B, Tp // tm, nk),
        in_specs=[pl.BlockSpec((tm, tk), lambda b, m, q: (m, q)),
                  pl.BlockSpec((None, tk, D), lambda b, m, q: (b, q, 0)),
                  pl.BlockSpec((None, tm, D), lambda b, m, q: (b, m, 0)),
                  _const_spec(w_f.shape)],
        out_specs=pl.BlockSpec((None, tm, D), lambda b, m, q: (b, m, 0)),
        out_shape=jax.ShapeDtypeStruct((B, Tp, D), F32),
        scratch_shapes=[pltpu.VMEM((tm, D), F32)],
        compiler_params=pltpu.CompilerParams(
            dimension_semantics=("parallel", "parallel", "arbitrary"), vmem_limit_bytes=VMEM_LIMIT_BYTES),
        name="fnet_t",
    )(wt, y.reshape(B, 2 * Tp, D), h, w_f)


def _dft_table(n, rows, cols):
    j = lax.broadcasted_iota(jnp.int32, (rows, cols), 0)
    k = lax.broadcasted_iota(jnp.int32, (rows, cols), 1)
    ang = ((j * k) % n).astype(F32) * (2.0 * math.pi / n)
    return jnp.cos(ang), jnp.sin(ang)


def _time_dft_matrix(tr, tp):
    nj1 = tp // CHUNK
    k1 = lax.broadcasted_iota(jnp.int32, (nj1, tp), 1)
    j1 = lax.broadcasted_iota(jnp.int32, (nj1, tp), 0)
    ang_a = ((CHUNK * j1 * k1) % tr).astype(F32) * (2.0 * math.pi / tr)
    ca, sa = jnp.cos(ang_a)[:, None, :], jnp.sin(ang_a)[:, None, :]
    cb, sb = _dft_table(tr, CHUNK, tp)
    cb, sb = cb[None], sb[None]
    j = lax.broadcasted_iota(jnp.int32, (tp, tp), 0)
    k = lax.broadcasted_iota(jnp.int32, (tp, tp), 1)
    ok = jnp.logical_and(j < tr, k < tr)
    scale = 1.0 / math.sqrt(tr)
    c = jnp.where(ok, (ca * cb - sa * sb).reshape(tp, tp) * scale, 0.0)
    s = jnp.where(ok, (sa * cb + ca * sb).reshape(tp, tp) * (-scale), 0.0)
    return jnp.concatenate([c, s], axis=1).astype(BF16)


def _largest_divisor(n, unit, cap):
    best = unit
    for m in range(unit, cap + 1, unit):
        if n % m == 0:
            best = m
    return best


BF16_ROWS = 16
LANES = 128
ROW_BLOCK_CAP = 448
DFT_FULL_K_CAP = 4608
DFT_K_BLOCK_CAP = 2048
DFT_M_BLOCK_CAP = 896
FFN_ROW_BLOCK_CAP = 896


def _tiles(tp):
    tb = _largest_divisor(tp, BF16_ROWS, ROW_BLOCK_CAP)
    full_k = 2 * tp <= DFT_FULL_K_CAP
    tk = 2 * tp if full_k else _largest_divisor(2 * tp, LANES, DFT_K_BLOCK_CAP)
    tm = _largest_divisor(tp, BF16_ROWS, ROW_BLOCK_CAP if full_k else DFT_M_BLOCK_CAP)
    tf = _largest_divisor(tp, NEXT_ROWS, FFN_ROW_BLOCK_CAP)
    return tb, tm, tk, tf


def _prepare(meta_tokens, norm_mix, norm_ffn, norm_final, rwkv_mu, rwkv_w_rkv, rwkv_w0, rwkv_w1, rwkv_w2,
             rwkv_a0, rwkv_a1, rwkv_a2, rwkv_g1, rwkv_g2, rwkv_k_k, rwkv_k_a, rwkv_r_k, rwkv_gn_w, rwkv_gn_b,
             rwkv_w_o, fnet_w_o, ffn_w_in, ffn_conv_w, ffn_conv_b, ffn_w_out):
    D = D_MODEL
    row = lambda x: x.reshape(1, -1).astype(F32)

    def lora_pair(w1, w2, w0):
        first = jnp.concatenate([w1[0], w1[1]], axis=1)
        z = jnp.zeros_like(w2[0])
        second = jnp.concatenate([jnp.concatenate([w2[0], z], axis=1),
                                  jnp.concatenate([z, w2[1]], axis=1)], axis=0)
        return first.astype(BF16), second.astype(BF16), jnp.concatenate([w0[0], w0[1]]).reshape(1, 2 * D)

    w1, w2, w0 = lora_pair(rwkv_w1[0], rwkv_w2[0], rwkv_w0[0])
    a1, a2, a0 = lora_pair(rwkv_a1[0], rwkv_a2[0], rwkv_a0[0])
    rank = rwkv_g1.shape[-1]
    g1 = jnp.pad(rwkv_g1[0], ((0, 0), (0, GATE_LORA_PAD - rank))).astype(BF16)
    g2 = jnp.pad(rwkv_g2[0], ((0, GATE_LORA_PAD - rank), (0, 0))).astype(BF16)
    head_of_lane = np.arange(D) // HEAD_SIZE
    e = (head_of_lane[:, None] == np.arange(128)[None, :]).astype(np.float32)
    cc, sc = _dft_table(FNET_GROUP_DIM, FNET_GROUP_DIM, FNET_GROUP_DIM)
    cs = (jnp.concatenate([cc, sc], axis=1) * (1.0 / math.sqrt(FNET_GROUP_DIM))).astype(BF16)
    tri_f = np.tril(np.ones((CHUNK, CHUNK), np.float32))
    return dict(
        meta=meta_tokens,
        norm_mix0=row(norm_mix[0]), norm_mix1=row(norm_mix[1]),
        norm_ffn0=row(norm_ffn[0]), norm_ffn1=row(norm_ffn[1]), norm_final=row(norm_final),
        mu=jnp.pad(rwkv_mu[0], ((0, 2), (0, 0))),
        w_rkv=rwkv_w_rkv[0].astype(BF16), w1=w1, w2=w2, w0=w0, a1=a1, a2=a2, a0=a0, g1=g1, g2=g2,
        k_k=row(rwkv_k_k[0]), k_a=row(rwkv_k_a[0]), r_k=row(rwkv_r_k[0]),
        gn_w=row(rwkv_gn_w[0]), gn_b=row(rwkv_gn_b[0]),
        w_o=rwkv_w_o[0].astype(BF16), w_f=fnet_w_o[0].astype(BF16),
        ffn_w_in=ffn_w_in.astype(BF16), ffn_conv_w=ffn_conv_w, ffn_conv_b=ffn_conv_b.reshape(-1, 1, D_FF),
        ffn_w_out=ffn_w_out.astype(BF16),
        E=jnp.asarray(e, BF16), ET=jnp.asarray(np.concatenate([e.T, e.T]), BF16), cs=cs,
        tri=jnp.asarray(np.stack([tri_f, tri_f.T]), BF16),
    )


def _trunk(x, p):
    B, T, D = x.shape
    tr = T + N_META
    tp = -(-tr // CHUNK) * CHUNK
    tb, tm, tk, tf = _tiles(tp)
    meta = jnp.broadcast_to(p['meta'].astype(x.dtype)[None], (B, N_META, D))
    h = jnp.concatenate([meta, x, jnp.zeros((B, tp - tr, D), x.dtype)], axis=1)

    r, k, v, kk, g, a, lw = _rwkv_proj(h, p, tb, tr)
    o_f, o_b = _wkv(r, k, v, kk, a, lw, p['k_a'], p['tri'])
    h = _rwkv_out(o_f, o_b, r, k, v, g, a, h, p, tb)
    h = _ffn(h, p['norm_ffn0'], p['ffn_w_in'][0], p['ffn_conv_w'][0], p['ffn_conv_b'][0], p['ffn_w_out'][0],
             p['norm_final'], tf, tr, out_rows=tp, final_norm=False)

    y = _fnet_c(h, p['norm_mix1'], p['cs'], tb, tr)
    h = _fnet_t(_time_dft_matrix(tr, tp), y, h, p['w_f'], tm, tk)
    h = _ffn(h, p['norm_ffn1'], p['ffn_w_in'][1], p['ffn_conv_w'][1], p['ffn_conv_b'][1], p['ffn_w_out'][1],
             p['norm_final'], tf, tr, out_rows=T, final_norm=True)
    return h


def kernel(x_prompt, x_sample, meta_tokens, norm_mix, norm_ffn, norm_final, rwkv_mu, rwkv_w_rkv, rwkv_w0, rwkv_w1, rwkv_w2, rwkv_a0, rwkv_a1, rwkv_a2, rwkv_g1, rwkv_g2, rwkv_k_k, rwkv_k_a, rwkv_r_k, rwkv_gn_w, rwkv_gn_b, rwkv_w_o, fnet_w_o, ffn_w_in, ffn_conv_w, ffn_conv_b, ffn_w_out):
    p = _prepare(meta_tokens, norm_mix, norm_ffn, norm_final, rwkv_mu, rwkv_w_rkv, rwkv_w0, rwkv_w1, rwkv_w2,
                 rwkv_a0, rwkv_a1, rwkv_a2, rwkv_g1, rwkv_g2, rwkv_k_k, rwkv_k_a, rwkv_r_k, rwkv_gn_w,
                 rwkv_gn_b, rwkv_w_o, fnet_w_o, ffn_w_in, ffn_conv_w, ffn_conv_b, ffn_w_out)
    return (_trunk(x_prompt, p), _trunk(x_sample, p))
```

```python
import functools
import math

import jax
import jax.numpy as jnp
import numpy as np
from jax import lax
from jax.experimental import pallas as pl
from jax.experimental.pallas import tpu as pltpu

D_MODEL = 1024
N_META = 16
HEAD_SIZE = 64
N_HEADS = D_MODEL // HEAD_SIZE
GATE_LORA_PAD = 256
LORA2 = 128
D_FF = 2816
FNET_GROUPS = 8
FNET_GROUP_DIM = D_MODEL // FNET_GROUPS
RMS_EPS = 1e-6
GN_EPS = 64e-5

CHUNK = 64
PAIR = 2 * HEAD_SIZE
N_PAIRS = N_HEADS // 2
HALO = 8
FF_CHUNK = 256
META_SHIFT = N_META
NEXT_ROWS = 32
WKV_SUBCHUNK_CAP = 5
VMEM_LIMIT_BYTES = 56 * 1024 * 1024

BF16 = jnp.bfloat16
F32 = jnp.float32


def _dot(a, b):
    return jnp.dot(a, b, preferred_element_type=F32)


def _dot_nt(a, b):
    return lax.dot_general(a, b, (((1,), (1,)), ((), ())), preferred_element_type=F32)


def _dot_tn(a, b):
    return lax.dot_general(a, b, (((0,), (0,)), ((), ())), preferred_element_type=F32)


def _rmsnorm(x, gain):
    ms = jnp.mean(x * x, axis=-1, keepdims=True)
    return x * lax.rsqrt(ms + RMS_EPS) * gain


def _split_bf16(x):
    hi = x.astype(BF16)
    lo = (x - hi.astype(F32)).astype(BF16)
    return hi, lo


def _head_sum(x, e_ref):
    return _dot(x.astype(BF16), e_ref[...])


def _head_bcast(s, et2_ref):
    hi, lo = _split_bf16(s)
    return _dot(jnp.concatenate([hi, lo], axis=1), et2_ref[...])


def _normed_window(h_ref, hp_ref, hn_ref, gain, i, nblk, tb, tr):
    row0 = i * tb
    rows = row0 + lax.broadcasted_iota(jnp.int32, (tb, 1), 0)
    valid = rows < tr
    x = jnp.where(valid, _rmsnorm(h_ref[...], gain), 0.0)
    xp = _rmsnorm(hp_ref[...], gain)[HALO - 1:HALO, :]
    xp = jnp.where(i > 0, xp, 0.0)
    xn = _rmsnorm(hn_ref[...], gain)[0:1, :]
    xn = jnp.where(jnp.logical_and(i < nblk - 1, row0 + tb < tr), xn, 0.0)
    return x, xp, xn, valid


def _shift_rows(x, xp, xn, tb):
    ridx = lax.broadcasted_iota(jnp.int32, (tb, 1), 0)
    x_prev = jnp.where(ridx == 0, xp, pltpu.roll(x, 1, 0))
    x_next = jnp.where(ridx == tb - 1, xn, pltpu.roll(x, tb - 1, 0))
    return x_prev, x_next


def _proj_kernel(h_ref, hp_ref, hn_ref, gain_ref, mu_ref, wrkv_ref, lora1_ref, w2_ref, w0_ref,
                 a2_ref, a0_ref, g2_ref, kk_ref, e_ref, et_ref,
                 r_out, k_out, v_out, kk_out, g_out, a_out, lw_out, *, tb, tr, nblk):
    i = pl.program_id(1)
    x, xp, xn, valid = _normed_window(h_ref, hp_ref, hn_ref, gain_ref[...], i, nblk, tb, tr)
    x_prev, x_next = _shift_rows(x, xp, xn, tb)
    xx = jnp.where(valid, 0.5 * (x_prev + x_next) - x, 0.0)
    mu = mu_ref[...]

    def mix(j):
        return (x + xx * mu[j:j + 1, :]).astype(BF16)

    k = _dot(mix(2), wrkv_ref[1])
    tw = _dot(mix(1), lora1_ref[:, :LORA2])
    ta = _dot(mix(4), lora1_ref[:, LORA2:2 * LORA2])
    tg = _dot(mix(5), lora1_ref[:, 2 * LORA2:])
    r = _dot(mix(0), wrkv_ref[0])
    kkr = k * kk_ref[...]
    ssq = _head_sum(kkr * kkr, e_ref)
    v = _dot(mix(3), wrkv_ref[2])
    w_lin = w0_ref[...] + _dot(jnp.tanh(tw).astype(BF16), w2_ref[...])
    a_lin = a0_ref[...] + _dot(ta.astype(BF16), a2_ref[...])
    gate = _dot(jax.nn.sigmoid(tg).astype(BF16), g2_ref[...])
    inv = _head_bcast(1.0 / jnp.maximum(jnp.sqrt(ssq), 1e-12), et_ref)

    k_out[...] = k.astype(BF16)
    r_out[...] = r.astype(BF16)
    v_out[...] = v.astype(BF16)
    lw_out[...] = (-math.exp(-0.5)) * jax.nn.sigmoid(w_lin)
    a_out[...] = jax.nn.sigmoid(a_lin).astype(BF16)
    g_out[...] = gate.astype(BF16)
    kk_out[...] = (kkr * inv).astype(BF16)


def _row_block_specs(tb, nhalo_blocks):
    per = tb // HALO
    main = pl.BlockSpec((None, tb, D_MODEL), lambda b, i, *_: (b, i, 0))
    prev = pl.BlockSpec((None, HALO, D_MODEL), lambda b, i, *_: (b, jnp.maximum(i * per - 1, 0), 0))
    nxt = pl.BlockSpec((None, HALO, D_MODEL),
                       lambda b, i, *_: (b, jnp.minimum((i + 1) * per, nhalo_blocks - 1), 0))
    return main, prev, nxt


def _const_spec(shape):
    nd = len(shape)
    return pl.BlockSpec(shape, lambda *_: (0,) * nd)


def _rwkv_proj(h, p, tb, tr):
    B, Tp, D = h.shape
    nblk = Tp // tb
    main, prev, nxt = _row_block_specs(tb, Tp // HALO)
    row = lambda w: pl.BlockSpec((None, tb, w), lambda b, i: (b, i, 0))
    consts = [p['norm_mix0'], p['mu'], p['w_rkv'], p['lora1'], p['w2'], p['w0'], p['a2'], p['a0'],
              p['g2'], p['k_k'], p['E'], p['ET']]
    out_shape = [jax.ShapeDtypeStruct((B, Tp, D), BF16)] * 5 + [
        jax.ShapeDtypeStruct((B, Tp, 2 * D), BF16), jax.ShapeDtypeStruct((B, Tp, 2 * D), F32)]
    return pl.pallas_call(
        functools.partial(_proj_kernel, tb=tb, tr=tr, nblk=nblk),
        grid=(B, nblk),
        in_specs=[main, prev, nxt] + [_const_spec(c.shape) for c in consts],
        out_specs=[row(D)] * 5 + [row(2 * D), row(2 * D)],
        out_shape=out_shape,
        compiler_params=pltpu.CompilerParams(
            dimension_semantics=("parallel", "arbitrary"), vmem_limit_bytes=VMEM_LIMIT_BYTES),
        name="rwkv_proj",
    )(h, h, h, *consts)


def _wkv_kernel(rf_ref, kf_ref, vf_ref, kkf_ref, af_ref, lwf_ref,
                rb_ref, kb_ref, vb_ref, kkb_ref, ab_ref, lwb_ref,
                ka_ref, tri_ref, of_ref, ob_ref, state_ref, *, nsub):
    c = pl.program_id(1)

    @pl.when(c == 0)
    def _():
        state_ref[...] = jnp.zeros_like(state_ref)

    for sub in range(nsub):
        _wkv_chunk(rf_ref, kf_ref, vf_ref, kkf_ref, af_ref, lwf_ref, rb_ref, kb_ref, vb_ref, kkb_ref, ab_ref,
                   lwb_ref, ka_ref, tri_ref, of_ref, ob_ref, state_ref,
                   pl.ds(sub * CHUNK, CHUNK), pl.ds((nsub - 1 - sub) * CHUNK, CHUNK))


def _wkv_chunk(rf_ref, kf_ref, vf_ref, kkf_ref, af_ref, lwf_ref, rb_ref, kb_ref, vb_ref, kkb_ref, ab_ref, lwb_ref,
               ka_ref, tri_ref, of_ref, ob_ref, state_ref, rows_f, rows_b):
    L = CHUNK
    t_idx = lax.broadcasted_iota(jnp.int32, (L, PAIR), 0)
    lane = lax.broadcasted_iota(jnp.int32, (L, PAIR), 1)
    s_idx = lane % HEAD_SIZE
    head0 = lax.broadcasted_iota(jnp.int32, (1, PAIR), 1) < HEAD_SIZE
    bd_mask = (lax.broadcasted_iota(jnp.int32, (PAIR, PAIR), 0) // HEAD_SIZE
               == lax.broadcasted_iota(jnp.int32, (PAIR, PAIR), 1) // HEAD_SIZE)
    ka = ka_ref[...]

    dirs = (
        (0, rows_f, rf_ref, kf_ref, vf_ref, kkf_ref, af_ref, lwf_ref, of_ref, s_idx < t_idx, s_idx <= t_idx, L - 1),
        (1, rows_b, rb_ref, kb_ref, vb_ref, kkb_ref, ab_ref, lwb_ref, ob_ref, s_idx > t_idx, s_idx >= t_idx, 0),
    )
    chains = []
    for d, rows, r_ref, k_ref, v_ref, kk_ref, a_ref, lw_ref, o_ref, strict, incl, last in dirs:
        lw = lw_ref[rows, :]
        hi, lo = _split_bf16(lw)
        cum = _dot(tri_ref[d], jnp.concatenate([hi, lo], axis=0))
        cum_last = cum[last:last + 1, :]
        g_incl = jnp.exp(cum)
        g_excl = jnp.exp(cum - lw)
        g_inv = jnp.exp(-cum)
        g_tail = jnp.exp(cum_last - cum)
        g_last = jnp.exp(cum_last)

        kk = kk_ref[rows, :].astype(F32)
        a = a_ref[rows, :].astype(F32)
        kdir = k_ref[rows, :].astype(F32) * (1.0 + (a - 1.0) * ka)
        b = kk * a
        a_t = (-kk * g_excl).astype(BF16)
        r_t = (r_ref[rows, :].astype(F32) * g_incl).astype(BF16)
        b_t = b * g_inv
        k_t = kdir * g_inv
        b_h = (b * g_tail).astype(BF16)
        k_h = (kdir * g_tail).astype(BF16)
        v = v_ref[rows, :]

        for p in range(N_PAIRS):
            sl = slice(p * PAIR, (p + 1) * PAIR)
            bp, kp, vp = b_t[:, sl], k_t[:, sl], v[:, sl]
            chains.append(dict(
                d=d, p=p, sl=sl, rows=rows, o_ref=o_ref, strict=strict, incl=incl, vp=vp,
                ar=jnp.concatenate([a_t[:, sl], r_t[:, sl]], axis=0),
                bk=jnp.concatenate([jnp.where(head0, bp, 0.0), jnp.where(head0, 0.0, bp),
                                    jnp.where(head0, kp, 0.0), jnp.where(head0, 0.0, kp)],
                                   axis=0).astype(BF16),
                v_bd=jnp.concatenate([jnp.where(head0, vp, 0), jnp.where(head0, 0, vp)], axis=0),
                bk_h=jnp.concatenate([b_h[:, sl], k_h[:, sl]], axis=0),
                g_last=g_last[:, sl]))

    for ch in chains:
        ch['s0'] = state_ref[ch['d'], ch['p']]
        ch['a_all'] = _dot_nt(ch['ar'], ch['bk'])
        ch['ah'] = _dot_nt(ch['ar'], ch['s0'].astype(BF16))
    for ch in chains:
        a_all, strict, incl = ch['a_all'], ch['strict'], ch['incl']
        ch['pk'] = jnp.where(strict, a_all[:L, :PAIR], 0.0)
        a_ak = jnp.where(strict, a_all[:L, PAIR:], 0.0).astype(BF16)
        ch['a_rbk'] = jnp.concatenate([jnp.where(incl, a_all[L:, :PAIR], 0.0),
                                       jnp.where(incl, a_all[L:, PAIR:], 0.0)], axis=1).astype(BF16)
        ch['w'] = ch['ah'][:L] + _dot(a_ak, ch['v_bd'])
    steps = int(math.log2(L))
    for step in range(steps):
        for ch in chains:
            pkb = ch['pk'].astype(BF16)
            wb = ch['w'].astype(BF16)
            if step < steps - 1:
                rhs = jnp.concatenate(
                    [jnp.concatenate([jnp.where(head0, pkb, 0), jnp.where(head0, wb, 0)], axis=1),
                     jnp.concatenate([jnp.where(head0, 0, pkb), jnp.where(head0, 0, wb)], axis=1)], axis=0)
                out = _dot(pkb, rhs)
                ch['pk'] = out[:, :PAIR]
                ch['w'] = ch['w'] + out[:, PAIR:]
            else:
                rhs = jnp.concatenate([jnp.where(head0, wb, 0), jnp.where(head0, 0, wb)], axis=0)
                ch['w'] = ch['w'] + _dot(pkb, rhs)
    for ch in chains:
        ub = ch['w'].astype(BF16)
        uv_bd = jnp.concatenate([jnp.where(head0, ub, 0), jnp.where(head0, 0, ub), ch['v_bd']], axis=0)
        ch['o_ref'][ch['rows'], ch['sl']] = (ch['ah'][L:] + _dot(ch['a_rbk'], uv_bd)).astype(ch['o_ref'].dtype)
        upd = _dot_tn(jnp.concatenate([ub, ch['vp']], axis=0), ch['bk_h'])
        state_ref[ch['d'], ch['p']] = ch['s0'] * ch['g_last'] + jnp.where(bd_mask, upd, 0.0)


def _wkv(r, k, v, kk, a, lw, k_a, tri):
    B, Tp, D = r.shape
    nsub = _largest_divisor(Tp // CHUNK, 1, WKV_SUBCHUNK_CAP)
    rows = nsub * CHUNK
    n = Tp // rows
    fwd = lambda col: pl.BlockSpec((None, rows, D), lambda b, c: (b, c, col))
    bwd = lambda col: pl.BlockSpec((None, rows, D), lambda b, c: (b, n - 1 - c, col))
    in_specs = ([fwd(0)] * 6 + [bwd(0)] * 4 + [bwd(1), bwd(1)]
                + [_const_spec(k_a.shape), _const_spec(tri.shape)])
    return pl.pallas_call(
        functools.partial(_wkv_kernel, nsub=nsub),
        grid=(B, n),
        in_specs=in_specs,
        out_specs=[fwd(0), bwd(0)],
        out_shape=[jax.ShapeDtypeStruct((B, Tp, D), BF16)] * 2,
        scratch_shapes=[pltpu.VMEM((2, N_PAIRS, PAIR, PAIR), F32)],
        compiler_params=pltpu.CompilerParams(
            dimension_semantics=("parallel", "arbitrary"), vmem_limit_bytes=VMEM_LIMIT_BYTES),
        name="wkv",
    )(r, k, v, kk, a, lw, r, k, v, kk, a, lw, k_a, tri)


def _rwkv_out_kernel(of_ref, ob_ref, r_ref, k_ref, v_ref, g_ref, af_ref, ab_ref, h_ref,
                     rk_ref, ka_ref, gnw_ref, gnb_ref, wo_ref, e_ref, et_ref, out_ref, *, tb):
    inv_n = 1.0 / HEAD_SIZE
    half = tb // 2
    parts = [dict(rows=slice(q * half, (q + 1) * half)) for q in range(2)]
    for pt in parts:
        rows = pt['rows']
        pt['o'] = of_ref[rows, :].astype(F32) + ob_ref[rows, :].astype(F32)
        a_sum = af_ref[rows, :].astype(F32) + ab_ref[rows, :].astype(F32)
        rk = (r_ref[rows, :].astype(F32) * rk_ref[...] * k_ref[rows, :].astype(F32)
              * (2.0 + (a_sum - 2.0) * ka_ref[...]))
        pt['s_o'] = _head_sum(pt['o'], e_ref)
        pt['s_rk'] = _head_sum(rk, e_ref)
    for pt in parts:
        pt['dlt'] = pt['o'] - _head_bcast(pt['s_o'] * inv_n, et_ref)
        pt['bonus'] = _head_bcast(pt['s_rk'], et_ref)
    for pt in parts:
        pt['s_v'] = _head_sum(pt['dlt'] * pt['dlt'], e_ref)
    for pt in parts:
        rows = pt['rows']
        var = _head_bcast(pt['s_v'] * inv_n, et_ref)
        on = pt['dlt'] * lax.rsqrt(var + GN_EPS) * gnw_ref[...] + gnb_ref[...]
        on = on + pt['bonus'] * v_ref[rows, :].astype(F32)
        y = (on * g_ref[rows, :].astype(F32)).astype(BF16)
        out_ref[rows, :] = h_ref[rows, :] + _dot(y, wo_ref[...])


def _rwkv_out(o_f, o_b, r, k, v, g, a, h, p, tb):
    B, Tp, D = h.shape
    row = lambda col=0: pl.BlockSpec((None, tb, D), lambda b, i: (b, i, col))
    consts = [p['r_k'], p['k_a'], p['gn_w'], p['gn_b'], p['w_o'], p['E'], p['ET']]
    return pl.pallas_call(
        functools.partial(_rwkv_out_kernel, tb=tb),
        grid=(B, Tp // tb),
        in_specs=[row()] * 6 + [row(0), row(1), row()] + [_const_spec(c.shape) for c in consts],
        out_specs=row(),
        out_shape=jax.ShapeDtypeStruct((B, Tp, D), F32),
        compiler_params=pltpu.CompilerParams(
            dimension_semantics=("parallel", "parallel"), vmem_limit_bytes=VMEM_LIMIT_BYTES),
        name="rwkv_out",
    )(o_f, o_b, r, k, v, g, a, a, h, *consts)


def _channel_dft(x, cs_ref, y_ref):
    cs = cs_ref[...]
    for g in range(FNET_GROUPS):
        sl = slice(g * FNET_GROUP_DIM, (g + 1) * FNET_GROUP_DIM)
        y = _dot(x[:, sl], cs)
        y_ref[0, :, sl] = y[:, :FNET_GROUP_DIM].astype(BF16)
        y_ref[1, :, sl] = y[:, FNET_GROUP_DIM:].astype(BF16)


def _ffn_kernel(h_ref, hp_ref, hn_ref, gain_ref, win_ref, cw_ref, cb_ref, wout_ref, gtail_ref, cs_ref, out_ref,
                *y_refs, tb, tr, shift, tail):
    i = pl.program_id(1)
    n_win = tb + 2 * HALO
    if shift == 0:
        hw = jnp.concatenate([hp_ref[...], h_ref[...], hn_ref[0:HALO, :]], axis=0)
        h_res = h_ref[...]
    else:
        hw = jnp.concatenate([h_ref[shift - HALO:, :], hn_ref[0:shift + HALO, :]], axis=0)
        h_res = jnp.concatenate([h_ref[shift:, :], hn_ref[0:shift, :]], axis=0)
    rows = i * tb + (shift - HALO) + lax.broadcasted_iota(jnp.int32, (n_win, 1), 0)
    valid = jnp.logical_and(rows >= 0, rows < tr)
    xw = jnp.where(valid, _rmsnorm(hw, gain_ref[...]), 0.0).astype(BF16)
    x_mid = xw[HALO:HALO + tb]

    def in_dots(c0, width):
        u_act = _dot(xw, win_ref[:, c0:c0 + width])
        u_lin = _dot(x_mid, win_ref[:, D_FF + c0:D_FF + c0 + width])
        return u_act, u_lin

    chunks = [(c0, min(FF_CHUNK, D_FF - c0)) for c0 in range(0, D_FF, FF_CHUNK)]
    acc = None
    nxt = in_dots(*chunks[0])
    for ci, (c0, width) in enumerate(chunks):
        u_act, u_lin = nxt
        if ci + 1 < len(chunks):
            nxt = in_dots(*chunks[ci + 1])
        cw = cw_ref[:, c0:c0 + width]
        c = (pltpu.roll(u_act, 1, 0)[HALO:HALO + tb] * cw[0:1, :]
             + u_act[HALO:HALO + tb] * cw[1:2, :]
             + pltpu.roll(u_act, n_win - 1, 0)[HALO:HALO + tb] * cw[2:3, :]
             + cb_ref[:, c0:c0 + width])
        y = (c * jax.nn.sigmoid(c) * u_lin).astype(BF16)
        part = _dot(y, wout_ref[c0:c0 + width, :])
        acc = part if acc is None else acc + part
    res = h_res + acc
    if tail == 'final_norm':
        res = _rmsnorm(res, gtail_ref[...])
    out_ref[...] = res
    if tail == 'fnet_c':
        out_rows = i * tb + lax.broadcasted_iota(jnp.int32, (tb, 1), 0)
        xn = jnp.where(out_rows < tr, _rmsnorm(res, gtail_ref[...]), 0.0).astype(BF16)
        _channel_dft(xn, cs_ref, y_refs[0])


def _ffn(h, gain, w_in, conv_w, conv_b, w_out, g_tail, cs, tb, tr, out_rows, tail):
    B, Tp, D = h.shape
    shift = 0 if out_rows == Tp else META_SHIFT
    assert tail in ('final_norm', 'fnet_c') and (tail != 'fnet_c' or shift == 0)
    out_specs = [pl.BlockSpec((None, tb, D), lambda b, i: (b, i, 0))]
    out_shape = [jax.ShapeDtypeStruct((B, out_rows, D), F32)]
    if tail == 'fnet_c':
        out_specs.append(pl.BlockSpec((None, 2, tb, D), lambda b, i: (b, 0, i, 0)))
        out_shape.append(jax.ShapeDtypeStruct((B, 2, Tp, D), BF16))
    nblk = pl.cdiv(out_rows, tb)
    per_prev, per_next, n_next = tb // HALO, tb // NEXT_ROWS, Tp // NEXT_ROWS
    resident = lambda shape: pl.BlockSpec(shape, lambda *_: (0,) * len(shape), pipeline_mode=pl.Buffered(1))
    in_specs = [
        pl.BlockSpec((None, tb, D), lambda b, i: (b, i, 0)),
        pl.BlockSpec((None, HALO, D), lambda b, i: (b, jnp.maximum(i * per_prev - 1, 0), 0)),
        pl.BlockSpec((None, NEXT_ROWS, D), lambda b, i: (b, jnp.minimum((i + 1) * per_next, n_next - 1), 0)),
        _const_spec(gain.shape), resident(w_in.shape), _const_spec(conv_w.shape), _const_spec(conv_b.shape),
        resident(w_out.shape), _const_spec(g_tail.shape), _const_spec(cs.shape),
    ]
    return pl.pallas_call(
        functools.partial(_ffn_kernel, tb=tb, tr=tr, shift=shift, tail=tail),
        grid=(B, nblk),
        in_specs=in_specs,
        out_specs=out_specs,
        out_shape=out_shape,
        compiler_params=pltpu.CompilerParams(
            dimension_semantics=("parallel", "parallel"), vmem_limit_bytes=VMEM_LIMIT_BYTES),
        name="ffn_" + tail,
    )(h, h, h, gain, w_in, conv_w, conv_b, w_out, g_tail, cs)


def _fnet_t_kernel(wt_ref, y_ref, h_ref, wf_ref, out_ref, acc_ref, *, nk):
    kq = pl.program_id(2)

    @pl.when(kq == 0)
    def _():
        acc_ref[...] = jnp.zeros_like(acc_ref)

    acc_ref[...] += _dot(wt_ref[...], y_ref[...])

    @pl.when(kq == nk - 1)
    def _():
        out_ref[...] = h_ref[...] + _dot(acc_ref[...].astype(BF16), wf_ref[...])


def _fnet_t(wt, y, h, w_f, tm, tk):
    B, Tp, D = h.shape
    nk = (2 * Tp) // tk
    return pl.pallas_call(
        functools.partial(_fnet_t_kernel, nk=nk),
        grid=(B, Tp // tm, nk),
        in_specs=[pl.BlockSpec((tm, tk), lambda b, m, q: (m, q)),
                  pl.BlockSpec((None, tk, D), lambda b, m, q: (b, q, 0)),
                  pl.BlockSpec((None, tm, D), lambda b, m, q: (b, m, 0)),
                  _const_spec(w_f.shape)],
        out_specs=pl.BlockSpec((None, tm, D), lambda b, m, q: (b, m, 0)),
        out_shape=jax.ShapeDtypeStruct((B, Tp, D), F32),
        scratch_shapes=[pltpu.VMEM((tm, D), F32)],
        compiler_params=pltpu.CompilerParams(
            dimension_semantics=("parallel", "parallel", "arbitrary"), vmem_limit_bytes=VMEM_LIMIT_BYTES),
        name="fnet_t",
    )(wt, y.reshape(B, 2 * Tp, D), h, w_f)


def _dft_table(n, rows, cols):
    j = lax.broadcasted_iota(jnp.int32, (rows, cols), 0)
    k = lax.broadcasted_iota(jnp.int32, (rows, cols), 1)
    ang = ((j * k) % n).astype(F32) * (2.0 * math.pi / n)
    return jnp.cos(ang), jnp.sin(ang)


def _time_dft_matrix(tr, tp):
    nj1 = tp // CHUNK
    k1 = lax.broadcasted_iota(jnp.int32, (nj1, tp), 1)
    j1 = lax.broadcasted_iota(jnp.int32, (nj1, tp), 0)
    ang_a = ((CHUNK * j1 * k1) % tr).astype(F32) * (2.0 * math.pi / tr)
    ca, sa = jnp.cos(ang_a)[:, None, :], jnp.sin(ang_a)[:, None, :]
    cb, sb = _dft_table(tr, CHUNK, tp)
    cb, sb = cb[None], sb[None]
    j = lax.broadcasted_iota(jnp.int32, (tp, tp), 0)
    k = lax.broadcasted_iota(jnp.int32, (tp, tp), 1)
    ok = jnp.logical_and(j < tr, k < tr)
    scale = 1.0 / math.sqrt(tr)
    c = jnp.where(ok, (ca * cb - sa * sb).reshape(tp, tp) * scale, 0.0)
    s = jnp.where(ok, (sa * cb + ca * sb).reshape(tp, tp) * (-scale), 0.0)
    return jnp.concatenate([c, s], axis=1).astype(BF16)


def _largest_divisor(n, unit, cap):
    best = unit
    for m in range(unit, cap + 1, unit):
        if n % m == 0:
            best = m
    return best


BF16_ROWS = 16
LANES = 128
ROW_BLOCK_CAP = 448
DFT_FULL_K_CAP = 4608
DFT_K_BLOCK_CAP = 2048
DFT_M_BLOCK_CAP = 896
FFN_ROW_BLOCK_CAP = 896


def _tiles(tp):
    tb = _largest_divisor(tp, BF16_ROWS, ROW_BLOCK_CAP)
    full_k = 2 * tp <= DFT_FULL_K_CAP
    tk = 2 * tp if full_k else _largest_divisor(2 * tp, LANES, DFT_K_BLOCK_CAP)
    tm = _largest_divisor(tp, BF16_ROWS, ROW_BLOCK_CAP if full_k else DFT_M_BLOCK_CAP)
    tf = _largest_divisor(tp, NEXT_ROWS, FFN_ROW_BLOCK_CAP)
    return tb, tm, tk, tf


def _prepare(meta_tokens, norm_mix, norm_ffn, norm_final, rwkv_mu, rwkv_w_rkv, rwkv_w0, rwkv_w1, rwkv_w2,
             rwkv_a0, rwkv_a1, rwkv_a2, rwkv_g1, rwkv_g2, rwkv_k_k, rwkv_k_a, rwkv_r_k, rwkv_gn_w, rwkv_gn_b,
             rwkv_w_o, fnet_w_o, ffn_w_in, ffn_conv_w, ffn_conv_b, ffn_w_out):
    D = D_MODEL
    row = lambda x: x.reshape(1, -1).astype(F32)

    def lora_pair(w1, w2, w0):
        first = jnp.concatenate([w1[0], w1[1]], axis=1)
        z = jnp.zeros_like(w2[0])
        second = jnp.concatenate([jnp.concatenate([w2[0], z], axis=1),
                                  jnp.concatenate([z, w2[1]], axis=1)], axis=0)
        return first, second.astype(BF16), jnp.concatenate([w0[0], w0[1]]).reshape(1, 2 * D)

    w1, w2, w0 = lora_pair(rwkv_w1[0], rwkv_w2[0], rwkv_w0[0])
    a1, a2, a0 = lora_pair(rwkv_a1[0], rwkv_a2[0], rwkv_a0[0])
    rank = rwkv_g1.shape[-1]
    g1 = jnp.pad(rwkv_g1[0], ((0, 0), (0, GATE_LORA_PAD - rank)))
    g2 = jnp.pad(rwkv_g2[0], ((0, GATE_LORA_PAD - rank), (0, 0))).astype(BF16)
    lora1 = jnp.concatenate([w1, a1, g1], axis=1).astype(BF16)
    head_of_lane = np.arange(D) // HEAD_SIZE
    e = (head_of_lane[:, None] == np.arange(128)[None, :]).astype(np.float32)
    cc, sc = _dft_table(FNET_GROUP_DIM, FNET_GROUP_DIM, FNET_GROUP_DIM)
    cs = (jnp.concatenate([cc, sc], axis=1) * (1.0 / math.sqrt(FNET_GROUP_DIM))).astype(BF16)
    tri_f = np.tril(np.ones((CHUNK, CHUNK), np.float32))
    return dict(
        meta=meta_tokens,
        norm_mix0=row(norm_mix[0]), norm_mix1=row(norm_mix[1]),
        norm_ffn0=row(norm_ffn[0]), norm_ffn1=row(norm_ffn[1]), norm_final=row(norm_final),
        mu=jnp.pad(rwkv_mu[0], ((0, 2), (0, 0))),
        w_rkv=rwkv_w_rkv[0].astype(BF16), lora1=lora1, w2=w2, w0=w0, a2=a2, a0=a0, g2=g2,
        k_k=row(rwkv_k_k[0]), k_a=row(rwkv_k_a[0]), r_k=row(rwkv_r_k[0]),
        gn_w=row(rwkv_gn_w[0]), gn_b=row(rwkv_gn_b[0]),
        w_o=rwkv_w_o[0].astype(BF16), w_f=fnet_w_o[0].astype(BF16),
        ffn_w_in=ffn_w_in.astype(BF16), ffn_conv_w=ffn_conv_w, ffn_conv_b=ffn_conv_b.reshape(-1, 1, D_FF),
        ffn_w_out=ffn_w_out.astype(BF16),
        E=jnp.asarray(e, BF16), ET=jnp.asarray(np.concatenate([e.T, e.T]), BF16), cs=cs,
        tri=jnp.asarray(np.stack([np.tile(tri_f, (1, 2)), np.tile(tri_f.T, (1, 2))]), BF16),
    )


def _trunk(x, p):
    B, T, D = x.shape
    tr = T + N_META
    tp = -(-tr // CHUNK) * CHUNK
    tb, tm, tk, tf = _tiles(tp)
    meta = jnp.broadcast_to(p['meta'].astype(x.dtype)[None], (B, N_META, D))
    h = jnp.concatenate([meta, x, jnp.zeros((B, tp - tr, D), x.dtype)], axis=1)

    r, k, v, kk, g, a, lw = _rwkv_proj(h, p, tb, tr)
    o_f, o_b = _wkv(r, k, v, kk, a, lw, p['k_a'], p['tri'])
    h = _rwkv_out(o_f, o_b, r, k, v, g, a, h, p, tb)
    h, y = _ffn(h, p['norm_ffn0'], p['ffn_w_in'][0], p['ffn_conv_w'][0], p['ffn_conv_b'][0], p['ffn_w_out'][0],
             p['norm_mix1'], p['cs'], tf, tr, out_rows=tp, tail='fnet_c')
    h = _fnet_t(_time_dft_matrix(tr, tp), y, h, p['w_f'], tm, tk)
    return _ffn(h, p['norm_ffn1'], p['ffn_w_in'][1], p['ffn_conv_w'][1], p['ffn_conv_b'][1], p['ffn_w_out'][1],
                p['norm_final'], p['cs'], tf, tr, out_rows=T, tail='final_norm')[0]


def kernel(x_prompt, x_sample, meta_tokens, norm_mix, norm_ffn, norm_final, rwkv_mu, rwkv_w_rkv, rwkv_w0, rwkv_w1, rwkv_w2, rwkv_a0, rwkv_a1, rwkv_a2, rwkv_g1, rwkv_g2, rwkv_k_k, rwkv_k_a, rwkv_r_k, rwkv_gn_w, rwkv_gn_b, rwkv_w_o, fnet_w_o, ffn_w_in, ffn_conv_w, ffn_conv_b, ffn_w_out):
    p = _prepare(meta_tokens, norm_mix, norm_ffn, norm_final, rwkv_mu, rwkv_w_rkv, rwkv_w0, rwkv_w1, rwkv_w2,
                 rwkv_a0, rwkv_a1, rwkv_a2, rwkv_g1, rwkv_g2, rwkv_k_k, rwkv_k_a, rwkv_r_k, rwkv_gn_w,
                 rwkv_gn_b, rwkv_w_o, fnet_w_o, ffn_w_in, ffn_conv_w, ffn_conv_b, ffn_w_out)
    return (_trunk(x_prompt, p), _trunk(x_sample, p))
```

```python
import functools
import math

import jax
import jax.numpy as jnp
import numpy as np
from jax import lax
from jax.experimental import pallas as pl
from jax.experimental.pallas import tpu as pltpu

D_MODEL = 1024
N_META = 16
HEAD_SIZE = 64
N_HEADS = D_MODEL // HEAD_SIZE
GATE_LORA_PAD = 256
LORA2 = 128
D_FF = 2816
FNET_GROUPS = 8
FNET_GROUP_DIM = D_MODEL // FNET_GROUPS
HALF_D = D_MODEL // 2
RMS_EPS = 1e-6
GN_EPS = 64e-5
LOG2_E = math.log2(math.e)

CHUNK = 64
PAIR = 2 * HEAD_SIZE
N_PAIRS = N_HEADS // 2
HALO = 8
FF_CHUNK = 256
META_SHIFT = N_META
NEXT_ROWS = 32
WKV_SUBCHUNK_CAP = 5
VMEM_LIMIT_BYTES = 56 * 1024 * 1024

BF16 = jnp.bfloat16
F32 = jnp.float32


def _dot(a, b):
    return jnp.dot(a, b, preferred_element_type=F32)


def _dot_nt(a, b):
    return lax.dot_general(a, b, (((1,), (1,)), ((), ())), preferred_element_type=F32)


def _dot_tn(a, b):
    return lax.dot_general(a, b, (((0,), (0,)), ((), ())), preferred_element_type=F32)


def _rmsnorm(x, gain):
    ms = jnp.mean(x * x, axis=-1, keepdims=True)
    return x * lax.rsqrt(ms + RMS_EPS) * gain


def _split_bf16(x):
    hi = x.astype(BF16)
    lo = (x - hi.astype(F32)).astype(BF16)
    return hi, lo


def _head_sum(x, e_ref):
    return _dot(x.astype(BF16), e_ref[...])


def _head_bcast(s, et2_ref):
    hi, lo = _split_bf16(s)
    return _dot(jnp.concatenate([hi, lo], axis=1), et2_ref[...])


def _normed_window(h_ref, hp_ref, hn_ref, gain, i, nblk, tb, tr):
    row0 = i * tb
    rows = row0 + lax.broadcasted_iota(jnp.int32, (tb, 1), 0)
    valid = rows < tr
    x = jnp.where(valid, _rmsnorm(h_ref[...], gain), 0.0)
    xp = _rmsnorm(hp_ref[...], gain)[HALO - 1:HALO, :]
    xp = jnp.where(i > 0, xp, 0.0)
    xn = _rmsnorm(hn_ref[...], gain)[0:1, :]
    xn = jnp.where(jnp.logical_and(i < nblk - 1, row0 + tb < tr), xn, 0.0)
    return x, xp, xn, valid


def _shift_rows(x, xp, xn, tb):
    ridx = lax.broadcasted_iota(jnp.int32, (tb, 1), 0)
    x_prev = jnp.where(ridx == 0, xp, pltpu.roll(x, 1, 0))
    x_next = jnp.where(ridx == tb - 1, xn, pltpu.roll(x, tb - 1, 0))
    return x_prev, x_next


def _proj_kernel(h_ref, hp_ref, hn_ref, gain_ref, mu_ref, wrkv_ref, lora1_ref, w2_ref, w0_ref,
                 a2_ref, a0_ref, g2_ref, kk_ref, e_ref, et_ref,
                 r_out, k_out, v_out, kk_out, g_out, a_out, lw_out, *, tb, tr, nblk):
    i = pl.program_id(1)
    x, xp, xn, valid = _normed_window(h_ref, hp_ref, hn_ref, gain_ref[...], i, nblk, tb, tr)
    x_prev, x_next = _shift_rows(x, xp, xn, tb)
    xx = jnp.where(valid, 0.5 * (x_prev + x_next) - x, 0.0)
    mu = mu_ref[...]

    def mix(j):
        return (x + xx * mu[j:j + 1, :]).astype(BF16)

    k = _dot(mix(2), wrkv_ref[1])
    tw = _dot(mix(1), lora1_ref[:, :LORA2])
    ta = _dot(mix(4), lora1_ref[:, LORA2:2 * LORA2])
    tg = _dot(mix(5), lora1_ref[:, 2 * LORA2:])
    r = _dot(mix(0), wrkv_ref[0])
    kkr = k * kk_ref[...]
    ssq = _head_sum(kkr * kkr, e_ref)
    v = _dot(mix(3), wrkv_ref[2])
    w_lin = w0_ref[...] + _dot(jnp.tanh(tw).astype(BF16), w2_ref[...])
    a_lin = a0_ref[...] + _dot(ta.astype(BF16), a2_ref[...])
    gate = _dot(jax.nn.sigmoid(tg).astype(BF16), g2_ref[...])
    inv = _head_bcast(1.0 / jnp.maximum(jnp.sqrt(ssq), 1e-12), et_ref)

    k_out[...] = k.astype(BF16)
    r_out[...] = r.astype(BF16)
    v_out[...] = v.astype(BF16)
    lw_out[...] = (-math.exp(-0.5) * LOG2_E) * jax.nn.sigmoid(w_lin)
    a_out[...] = jax.nn.sigmoid(a_lin).astype(BF16)
    g_out[...] = gate.astype(BF16)
    kk_out[...] = (kkr * inv).astype(BF16)


def _row_block_specs(tb, nhalo_blocks):
    per = tb // HALO
    main = pl.BlockSpec((None, tb, D_MODEL), lambda b, i, *_: (b, i, 0))
    prev = pl.BlockSpec((None, HALO, D_MODEL), lambda b, i, *_: (b, jnp.maximum(i * per - 1, 0), 0))
    nxt = pl.BlockSpec((None, HALO, D_MODEL),
                       lambda b, i, *_: (b, jnp.minimum((i + 1) * per, nhalo_blocks - 1), 0))
    return main, prev, nxt


def _const_spec(shape):
    nd = len(shape)
    return pl.BlockSpec(shape, lambda *_: (0,) * nd)


def _rwkv_proj(h, p, tb, tr):
    B, Tp, D = h.shape
    nblk = Tp // tb
    main, prev, nxt = _row_block_specs(tb, Tp // HALO)
    row = lambda w: pl.BlockSpec((None, tb, w), lambda b, i: (b, i, 0))
    consts = [p['norm_mix0'], p['mu'], p['w_rkv'], p['lora1'], p['w2'], p['w0'], p['a2'], p['a0'],
              p['g2'], p['k_k'], p['E'], p['ET']]
    out_shape = [jax.ShapeDtypeStruct((B, Tp, D), BF16)] * 5 + [
        jax.ShapeDtypeStruct((B, Tp, 2 * D), BF16), jax.ShapeDtypeStruct((B, Tp, 2 * D), F32)]
    return pl.pallas_call(
        functools.partial(_proj_kernel, tb=tb, tr=tr, nblk=nblk),
        grid=(B, nblk),
        in_specs=[main, prev, nxt] + [_const_spec(c.shape) for c in consts],
        out_specs=[row(D)] * 5 + [row(2 * D), row(2 * D)],
        out_shape=out_shape,
        compiler_params=pltpu.CompilerParams(
            dimension_semantics=("parallel", "arbitrary"), vmem_limit_bytes=VMEM_LIMIT_BYTES),
        name="rwkv_proj",
    )(h, h, h, *consts)


def _wkv_kernel(rf_ref, kf_ref, vf_ref, kkf_ref, af_ref, lwf_ref,
                rb_ref, kb_ref, vb_ref, kkb_ref, ab_ref, lwb_ref,
                ka_ref, tri_ref, of_ref, ob_ref, state_ref, *, nsub):
    c = pl.program_id(1)

    @pl.when(c == 0)
    def _():
        state_ref[...] = jnp.zeros_like(state_ref)

    for sub in range(nsub):
        chains = _wkv_prep(rf_ref, kf_ref, vf_ref, kkf_ref, af_ref, lwf_ref, rb_ref, kb_ref, vb_ref, kkb_ref,
                           ab_ref, lwb_ref, ka_ref, tri_ref, of_ref, ob_ref,
                           pl.ds(sub * CHUNK, CHUNK), pl.ds((nsub - 1 - sub) * CHUNK, CHUNK))
        for ch in chains:
            _wkv_scores(ch)
        _wkv_solve(chains, state_ref)


def _wkv_scores(ch):
    ch['a_all'] = _dot_nt(ch['ar'], ch['bk'])


def _wkv_prep(rf_ref, kf_ref, vf_ref, kkf_ref, af_ref, lwf_ref, rb_ref, kb_ref, vb_ref, kkb_ref, ab_ref, lwb_ref,
              ka_ref, tri_ref, of_ref, ob_ref, rows_f, rows_b):
    L = CHUNK
    t_idx = lax.broadcasted_iota(jnp.int32, (L, PAIR), 0)
    lane = lax.broadcasted_iota(jnp.int32, (L, PAIR), 1)
    s_idx = lane % HEAD_SIZE
    head0 = lax.broadcasted_iota(jnp.int32, (1, PAIR), 1) < HEAD_SIZE
    ka = ka_ref[...]
    one_minus_ka = 1.0 - ka

    dirs = (
        (0, rows_f, rf_ref, kf_ref, vf_ref, kkf_ref, af_ref, lwf_ref, of_ref, s_idx < t_idx, s_idx <= t_idx, L - 1),
        (1, rows_b, rb_ref, kb_ref, vb_ref, kkb_ref, ab_ref, lwb_ref, ob_ref, s_idx > t_idx, s_idx >= t_idx, 0),
    )
    chains = []
    for d, rows, r_ref, k_ref, v_ref, kk_ref, a_ref, lw_ref, o_ref, strict, incl, last in dirs:
        lw = lw_ref[rows, :]
        hi, lo = _split_bf16(lw)
        cum = _dot(tri_ref[d], jnp.concatenate([hi, lo], axis=0))
        cum_last = cum[last:last + 1, :]
        g_incl = jnp.exp2(cum)
        g_excl = jnp.exp2(cum - lw)
        g_inv = jnp.exp2(-cum)
        g_tail = jnp.exp2(cum_last - cum)
        g_last = jnp.exp2(cum_last)

        kk = kk_ref[rows, :].astype(F32)
        a = a_ref[rows, :].astype(F32)
        kdir = k_ref[rows, :].astype(F32) * (a * ka + one_minus_ka)
        b = kk * a
        a_t = (-kk * g_excl).astype(BF16)
        r_t = (r_ref[rows, :].astype(F32) * g_incl).astype(BF16)
        b_t = b * g_inv
        k_t = kdir * g_inv
        b_h = (b * g_tail).astype(BF16)
        k_h = (kdir * g_tail).astype(BF16)
        v = v_ref[rows, :]

        for p in range(N_PAIRS):
            sl = slice(p * PAIR, (p + 1) * PAIR)
            bp, kp, vp = b_t[:, sl], k_t[:, sl], v[:, sl]
            chains.append(dict(
                d=d, p=p, sl=sl, rows=rows, o_ref=o_ref, strict=strict, incl=incl, vp=vp,
                ar=jnp.concatenate([a_t[:, sl], r_t[:, sl]], axis=0),
                bk=jnp.concatenate([jnp.where(head0, bp, 0.0), jnp.where(head0, 0.0, bp),
                                    jnp.where(head0, kp, 0.0), jnp.where(head0, 0.0, kp)],
                                   axis=0).astype(BF16),
                v_bd=jnp.concatenate([jnp.where(head0, vp, 0), jnp.where(head0, 0, vp)], axis=0),
                bk_h=jnp.concatenate([b_h[:, sl], k_h[:, sl]], axis=0),
                g_last=g_last[:, sl]))
    return chains


def _wkv_solve(chains, state_ref):
    L = CHUNK
    head0 = lax.broadcasted_iota(jnp.int32, (1, PAIR), 1) < HEAD_SIZE
    steps = int(math.log2(L))
    for ch in chains:
        ch['s0'] = state_ref[ch['d'], ch['p']]
        s0b = ch['s0'].astype(BF16)
        s0b = jnp.concatenate([jnp.where(head0, s0b[:HEAD_SIZE], 0), jnp.where(head0, 0, s0b[HEAD_SIZE:])], axis=0)
        ch['ah'] = _dot_nt(ch['ar'], s0b)
    for ch in chains:
        a_all, strict, incl = ch['a_all'], ch['strict'], ch['incl']
        ch['pk'] = jnp.where(strict, a_all[:L, :PAIR], 0.0)
        a_ak = jnp.where(strict, a_all[:L, PAIR:], 0.0).astype(BF16)
        ch['a_rbk'] = jnp.concatenate([jnp.where(incl, a_all[L:, :PAIR], 0.0),
                                       jnp.where(incl, a_all[L:, PAIR:], 0.0)], axis=1).astype(BF16)
        ch['w'] = ch['ah'][:L] + _dot(a_ak, ch['v_bd'])
    for step in range(steps):
        for ch in chains:
            pkb = ch['pk'].astype(BF16)
            wb = ch['w'].astype(BF16)
            if step < steps - 1:
                rhs = jnp.concatenate(
                    [jnp.concatenate([jnp.where(head0, pkb, 0), jnp.where(head0, wb, 0)], axis=1),
                     jnp.concatenate([jnp.where(head0, 0, pkb), jnp.where(head0, 0, wb)], axis=1)], axis=0)
                out = _dot(pkb, rhs)
                ch['pk'] = out[:, :PAIR]
                ch['w'] = ch['w'] + out[:, PAIR:]
            else:
                rhs = jnp.concatenate([jnp.where(head0, wb, 0), jnp.where(head0, 0, wb)], axis=0)
                ch['w'] = ch['w'] + _dot(pkb, rhs)
    for ch in chains:
        ub = ch['w'].astype(BF16)
        uv_bd = jnp.concatenate([jnp.where(head0, ub, 0), jnp.where(head0, 0, ub), ch['v_bd']], axis=0)
        ch['o_ref'][ch['rows'], ch['sl']] = (ch['ah'][L:] + _dot(ch['a_rbk'], uv_bd)).astype(ch['o_ref'].dtype)
        upd = _dot_tn(jnp.concatenate([ub, ch['vp']], axis=0), ch['bk_h'])
        state_ref[ch['d'], ch['p']] = ch['s0'] * ch['g_last'] + upd


def _wkv(r, k, v, kk, a, lw, k_a, tri):
    B, Tp, D = r.shape
    nsub = _largest_divisor(Tp // CHUNK, 1, WKV_SUBCHUNK_CAP)
    rows = nsub * CHUNK
    n = Tp // rows
    fwd = lambda col: pl.BlockSpec((None, rows, D), lambda b, c: (b, c, col))
    bwd = lambda col: pl.BlockSpec((None, rows, D), lambda b, c: (b, n - 1 - c, col))
    in_specs = ([fwd(0)] * 6 + [bwd(0)] * 4 + [bwd(1), bwd(1)]
                + [_const_spec(k_a.shape), _const_spec(tri.shape)])
    return pl.pallas_call(
        functools.partial(_wkv_kernel, nsub=nsub),
        grid=(B, n),
        in_specs=in_specs,
        out_specs=[fwd(0), bwd(0)],
        out_shape=[jax.ShapeDtypeStruct((B, Tp, D), BF16)] * 2,
        scratch_shapes=[pltpu.VMEM((2, N_PAIRS, PAIR, PAIR), F32)],
        compiler_params=pltpu.CompilerParams(
            dimension_semantics=("parallel", "arbitrary"), vmem_limit_bytes=VMEM_LIMIT_BYTES),
        name="wkv",
    )(r, k, v, kk, a, lw, r, k, v, kk, a, lw, k_a, tri)


def _rwkv_out_kernel(of_ref, ob_ref, r_ref, k_ref, v_ref, g_ref, af_ref, ab_ref, h_ref,
                     rk_ref, ka_ref, gnw_ref, gnb_ref, wo_ref, e_ref, et_ref, out_ref, *, tb):
    inv_n = 1.0 / HEAD_SIZE
    half = tb // 2
    parts = [dict(rows=slice(q * half, (q + 1) * half)) for q in range(2)]
    for pt in parts:
        rows = pt['rows']
        pt['o'] = of_ref[rows, :].astype(F32) + ob_ref[rows, :].astype(F32)
        a_sum = af_ref[rows, :].astype(F32) + ab_ref[rows, :].astype(F32)
        rk = (r_ref[rows, :].astype(F32) * rk_ref[...] * k_ref[rows, :].astype(F32)
              * (2.0 + (a_sum - 2.0) * ka_ref[...]))
        pt['s_o'] = _head_sum(pt['o'], e_ref)
        pt['s_rk'] = _head_sum(rk, e_ref)
    for pt in parts:
        pt['dlt'] = pt['o'] - _head_bcast(pt['s_o'] * inv_n, et_ref)
        pt['bonus'] = _head_bcast(pt['s_rk'], et_ref)
    for pt in parts:
        pt['s_v'] = _head_sum(pt['dlt'] * pt['dlt'], e_ref)
    for pt in parts:
        rows = pt['rows']
        var = _head_bcast(pt['s_v'] * inv_n, et_ref)
        on = pt['dlt'] * lax.rsqrt(var + GN_EPS) * gnw_ref[...] + gnb_ref[...]
        on = on + pt['bonus'] * v_ref[rows, :].astype(F32)
        y = (on * g_ref[rows, :].astype(F32)).astype(BF16)
        out_ref[rows, :] = h_ref[rows, :] + _dot(y, wo_ref[...])


def _rwkv_out(o_f, o_b, r, k, v, g, a, h, p, tb):
    B, Tp, D = h.shape
    row = lambda col=0: pl.BlockSpec((None, tb, D), lambda b, i: (b, i, col))
    consts = [p['r_k'], p['k_a'], p['gn_w'], p['gn_b'], p['w_o'], p['E'], p['ET']]
    return pl.pallas_call(
        functools.partial(_rwkv_out_kernel, tb=tb),
        grid=(B, Tp // tb),
        in_specs=[row()] * 6 + [row(0), row(1), row()] + [_const_spec(c.shape) for c in consts],
        out_specs=row(),
        out_shape=jax.ShapeDtypeStruct((B, Tp, D), F32),
        compiler_params=pltpu.CompilerParams(
            dimension_semantics=("parallel", "parallel"), vmem_limit_bytes=VMEM_LIMIT_BYTES),
        name="rwkv_out",
    )(o_f, o_b, r, k, v, g, a, a, h, *consts)


def _channel_dft(x, cs2_ref, nyq_ref, y_ref, ynyq_ref):
    cs2 = cs2_ref[...]
    for q in range(FNET_GROUPS // 2):
        y = _dot(x[:, q * 2 * FNET_GROUP_DIM:(q + 1) * 2 * FNET_GROUP_DIM], cs2).astype(BF16)
        y_ref[:, q * FNET_GROUP_DIM:(q + 1) * FNET_GROUP_DIM] = y[:, :FNET_GROUP_DIM]
        y_ref[:, HALF_D + q * FNET_GROUP_DIM:HALF_D + (q + 1) * FNET_GROUP_DIM] = y[:, FNET_GROUP_DIM:]
    ynyq_ref[...] = _dot(x, nyq_ref[...]).astype(BF16)


def _ffn_kernel(h_ref, hp_ref, hn_ref, gain_ref, win_ref, cw_ref, cb_ref, wout_ref, gtail_ref, cs2_ref, nyq_ref,
                out_ref, *y_refs, tb, tr, shift, tail):
    i = pl.program_id(1)
    n_win = tb + 2 * HALO
    if shift == 0:
        hw = jnp.concatenate([hp_ref[...], h_ref[...], hn_ref[0:HALO, :]], axis=0)
        h_res = h_ref[...]
    else:
        hw = jnp.concatenate([h_ref[shift - HALO:, :], hn_ref[0:shift + HALO, :]], axis=0)
        h_res = jnp.concatenate([h_ref[shift:, :], hn_ref[0:shift, :]], axis=0)
    rows = i * tb + (shift - HALO) + lax.broadcasted_iota(jnp.int32, (n_win, 1), 0)
    valid = jnp.logical_and(rows >= 0, rows < tr)
    xw = jnp.where(valid, _rmsnorm(hw, gain_ref[...]), 0.0).astype(BF16)
    x_mid = xw[HALO:HALO + tb]

    def in_dots(c0, width):
        u_act = _dot(xw, win_ref[:, c0:c0 + width])
        u_lin = _dot(x_mid, win_ref[:, D_FF + c0:D_FF + c0 + width])
        return u_act, u_lin

    chunks = [(c0, min(FF_CHUNK, D_FF - c0)) for c0 in range(0, D_FF, FF_CHUNK)]
    acc = None
    nxt = in_dots(*chunks[0])
    for ci, (c0, width) in enumerate(chunks):
        u_act, u_lin = nxt
        if ci + 1 < len(chunks):
            nxt = in_dots(*chunks[ci + 1])
        cw = cw_ref[:, c0:c0 + width]
        c = (pltpu.roll(u_act, 1, 0)[HALO:HALO + tb] * cw[0:1, :]
             + u_act[HALO:HALO + tb] * cw[1:2, :]
             + pltpu.roll(u_act, n_win - 1, 0)[HALO:HALO + tb] * cw[2:3, :]
             + cb_ref[:, c0:c0 + width])
        y = (c * jax.nn.sigmoid(c) * u_lin).astype(BF16)
        part = _dot(y, wout_ref[c0:c0 + width, :])
        acc = part if acc is None else acc + part
    res = h_res + acc
    if tail == 'final_norm':
        res = _rmsnorm(res, gtail_ref[...])
    out_ref[...] = res
    if tail == 'fnet_c':
        out_rows = i * tb + lax.broadcasted_iota(jnp.int32, (tb, 1), 0)
        xn = jnp.where(out_rows < tr, _rmsnorm(res, gtail_ref[...]), 0.0).astype(BF16)
        _channel_dft(xn, cs2_ref, nyq_ref, *y_refs)


def _ffn(h, gain, w_in, conv_w, conv_b, w_out, g_tail, cs2, nyq, tb, tr, out_rows, tail):
    B, Tp, D = h.shape
    shift = 0 if out_rows == Tp else META_SHIFT
    assert tail in ('final_norm', 'fnet_c') and (tail != 'fnet_c' or shift == 0)
    out_specs = [pl.BlockSpec((None, tb, D), lambda b, i: (b, i, 0))]
    out_shape = [jax.ShapeDtypeStruct((B, out_rows, D), F32)]
    if tail == 'fnet_c':
        out_specs += [pl.BlockSpec((None, tb, D), lambda b, i: (b, i, 0)),
                      pl.BlockSpec((None, tb, LANES), lambda b, i: (b, i, 0))]
        out_shape += [jax.ShapeDtypeStruct((B, Tp, D), BF16), jax.ShapeDtypeStruct((B, Tp, LANES), BF16)]
    nblk = pl.cdiv(out_rows, tb)
    per_prev, per_next, n_next = tb // HALO, tb // NEXT_ROWS, Tp // NEXT_ROWS
    resident = lambda shape: pl.BlockSpec(shape, lambda *_: (0,) * len(shape), pipeline_mode=pl.Buffered(1))
    in_specs = [
        pl.BlockSpec((None, tb, D), lambda b, i: (b, i, 0)),
        pl.BlockSpec((None, HALO, D), lambda b, i: (b, jnp.maximum(i * per_prev - 1, 0), 0)),
        pl.BlockSpec((None, NEXT_ROWS, D), lambda b, i: (b, jnp.minimum((i + 1) * per_next, n_next - 1), 0)),
        _const_spec(gain.shape), resident(w_in.shape), _const_spec(conv_w.shape), _const_spec(conv_b.shape),
        resident(w_out.shape), _const_spec(g_tail.shape), _const_spec(cs2.shape), _const_spec(nyq.shape),
    ]
    return pl.pallas_call(
        functools.partial(_ffn_kernel, tb=tb, tr=tr, shift=shift, tail=tail),
        grid=(B, nblk),
        in_specs=in_specs,
        out_specs=out_specs,
        out_shape=out_shape,
        compiler_params=pltpu.CompilerParams(
            dimension_semantics=("parallel", "parallel"), vmem_limit_bytes=VMEM_LIMIT_BYTES),
        name="ffn_" + tail,
    )(h, h, h, gain, w_in, conv_w, conv_b, w_out, g_tail, cs2, nyq)


def _fnet_t_kernel(ct_ref, st_ref, y_ref, ynyq_ref, h_ref, wf_ref, wn_ref, out_ref):
    ct = ct_ref[...]
    zc = _dot(ct, y_ref[:, :HALF_D])
    zs = _dot(st_ref[...], y_ref[:, HALF_D:])
    zn = _dot(ct, ynyq_ref[...])
    z = jnp.concatenate([zc, zs], axis=1).astype(BF16)
    out_ref[...] = h_ref[...] + _dot(z, wf_ref[...]) + _dot(zn.astype(BF16), wn_ref[...])


def _fnet_t(ct, st, y, ynyq, h, wf, wn, tm):
    B, Tp, D = h.shape
    return pl.pallas_call(
        _fnet_t_kernel,
        grid=(B, Tp // tm),
        in_specs=[pl.BlockSpec((tm, Tp), lambda b, m: (m, 0)),
                  pl.BlockSpec((tm, Tp), lambda b, m: (m, 0)),
                  pl.BlockSpec((None, Tp, D), lambda b, m: (b, 0, 0)),
                  pl.BlockSpec((None, Tp, LANES), lambda b, m: (b, 0, 0)),
                  pl.BlockSpec((None, tm, D), lambda b, m: (b, m, 0)),
                  _const_spec(wf.shape), _const_spec(wn.shape)],
        out_specs=pl.BlockSpec((None, tm, D), lambda b, m: (b, m, 0)),
        out_shape=jax.ShapeDtypeStruct((B, Tp, D), F32),
        compiler_params=pltpu.CompilerParams(
            dimension_semantics=("parallel", "parallel"), vmem_limit_bytes=VMEM_LIMIT_BYTES),
        name="fnet_t",
    )(ct, st, y, ynyq, h, wf, wn)


def _dft_table(n, rows, cols):
    j = lax.broadcasted_iota(jnp.int32, (rows, cols), 0)
    k = lax.broadcasted_iota(jnp.int32, (rows, cols), 1)
    ang = ((j * k) % n).astype(F32) * (2.0 * math.pi / n)
    return jnp.cos(ang), jnp.sin(ang)


def _time_dft_matrices(tr, tp):
    nj1 = tp // CHUNK
    k1 = lax.broadcasted_iota(jnp.int32, (nj1, tp), 1)
    j1 = lax.broadcasted_iota(jnp.int32, (nj1, tp), 0)
    ang_a = ((CHUNK * j1 * k1) % tr).astype(F32) * (2.0 * math.pi / tr)
    ca, sa = jnp.cos(ang_a)[:, None, :], jnp.sin(ang_a)[:, None, :]
    cb, sb = _dft_table(tr, CHUNK, tp)
    cb, sb = cb[None], sb[None]
    j = lax.broadcasted_iota(jnp.int32, (tp, tp), 0)
    k = lax.broadcasted_iota(jnp.int32, (tp, tp), 1)
    ok = jnp.logical_and(j < tr, k < tr)
    scale = 1.0 / math.sqrt(tr)
    c = jnp.where(ok, (ca * cb - sa * sb).reshape(tp, tp) * scale, 0.0)
    s = jnp.where(ok, (sa * cb + ca * sb).reshape(tp, tp) * scale, 0.0)
    return c.astype(BF16), s.astype(BF16)


def _channel_dft_tables():
    n, half = FNET_GROUP_DIM, FNET_GROUP_DIM // 2
    c = np.arange(n)[:, None]
    m = np.arange(half)[None, :]
    cos = np.cos(2 * np.pi * ((c * m) % n) / n)
    sin = np.sin(2 * np.pi * ((c * m) % n) / n)
    alt = np.where(np.arange(n) % 2 == 0, 1.0, -1.0)
    sin[:, 0] = alt
    cs2 = np.zeros((2 * n, 2 * n))
    for q in range(2):
        cs2[q * n:(q + 1) * n, q * half:(q + 1) * half] = cos
        cs2[q * n:(q + 1) * n, n + q * half:n + (q + 1) * half] = sin
    nyq = np.zeros((D_MODEL, LANES))
    for g in range(FNET_GROUPS):
        nyq[g * n:(g + 1) * n, g] = alt
    scale = 1.0 / math.sqrt(n)
    return jnp.asarray(cs2 * scale, BF16), jnp.asarray(nyq * scale, BF16)


def _fold_fnet_weight(w_f):
    n, half = FNET_GROUP_DIM, FNET_GROUP_DIM // 2
    w = w_f.reshape(FNET_GROUPS, n, D_MODEL)
    mirror = jnp.pad(jnp.flip(w[:, half + 1:], axis=1), ((0, 0), (1, 0), (0, 0)))
    w_cos = w[:, :half] + mirror
    w_sin = (mirror - w[:, :half]).at[:, 0].set(0.0)
    wf = jnp.concatenate([w_cos.reshape(HALF_D, D_MODEL), w_sin.reshape(HALF_D, D_MODEL)], axis=0)
    wn = jnp.pad(w[:, half], ((0, LANES - FNET_GROUPS), (0, 0)))
    return wf.astype(BF16), wn.astype(BF16)


def _largest_divisor(n, unit, cap):
    best = unit
    for m in range(unit, cap + 1, unit):
        if n % m == 0:
            best = m
    return best


BF16_ROWS = 16
LANES = 128
ROW_BLOCK_CAP = 448
DFT_BLOCK_BYTES_CAP = 7 * 1024 * 1024 // 2
FFN_ROW_BLOCK_CAP = 896


def _tiles(tp):
    tb = _largest_divisor(tp, BF16_ROWS, ROW_BLOCK_CAP)
    tm = _largest_divisor(tp, BF16_ROWS, max(BF16_ROWS, DFT_BLOCK_BYTES_CAP // (2 * tp)))
    tf = _largest_divisor(tp, NEXT_ROWS, FFN_ROW_BLOCK_CAP)
    return tb, tm, tf


def _prepare(meta_tokens, norm_mix, norm_ffn, norm_final, rwkv_mu, rwkv_w_rkv, rwkv_w0, rwkv_w1, rwkv_w2,
             rwkv_a0, rwkv_a1, rwkv_a2, rwkv_g1, rwkv_g2, rwkv_k_k, rwkv_k_a, rwkv_r_k, rwkv_gn_w, rwkv_gn_b,
             rwkv_w_o, fnet_w_o, ffn_w_in, ffn_conv_w, ffn_conv_b, ffn_w_out):
    D = D_MODEL
    row = lambda x: x.reshape(1, -1).astype(F32)

    def lora_pair(w1, w2, w0):
        first = jnp.concatenate([w1[0], w1[1]], axis=1)
        z = jnp.zeros_like(w2[0])
        second = jnp.concatenate([jnp.concatenate([w2[0], z], axis=1),
                                  jnp.concatenate([z, w2[1]], axis=1)], axis=0)
        return first, second.astype(BF16), jnp.concatenate([w0[0], w0[1]]).reshape(1, 2 * D)

    w1, w2, w0 = lora_pair(rwkv_w1[0], rwkv_w2[0], rwkv_w0[0])
    a1, a2, a0 = lora_pair(rwkv_a1[0], rwkv_a2[0], rwkv_a0[0])
    rank = rwkv_g1.shape[-1]
    g1 = jnp.pad(rwkv_g1[0], ((0, 0), (0, GATE_LORA_PAD - rank)))
    g2 = jnp.pad(rwkv_g2[0], ((0, GATE_LORA_PAD - rank), (0, 0))).astype(BF16)
    lora1 = jnp.concatenate([w1, a1, g1], axis=1).astype(BF16)
    head_of_lane = np.arange(D) // HEAD_SIZE
    e = (head_of_lane[:, None] == np.arange(128)[None, :]).astype(np.float32)
    cs2, nyq = _channel_dft_tables()
    wf, wn = _fold_fnet_weight(fnet_w_o[0])
    tri_f = np.tril(np.ones((CHUNK, CHUNK), np.float32))
    return dict(
        meta=meta_tokens,
        norm_mix0=row(norm_mix[0]), norm_mix1=row(norm_mix[1]),
        norm_ffn0=row(norm_ffn[0]), norm_ffn1=row(norm_ffn[1]), norm_final=row(norm_final),
        mu=jnp.pad(rwkv_mu[0], ((0, 2), (0, 0))),
        w_rkv=rwkv_w_rkv[0].astype(BF16), lora1=lora1, w2=w2, w0=w0, a2=a2, a0=a0, g2=g2,
        k_k=row(rwkv_k_k[0]), k_a=row(rwkv_k_a[0]), r_k=row(rwkv_r_k[0]),
        gn_w=row(rwkv_gn_w[0]), gn_b=row(rwkv_gn_b[0]),
        w_o=rwkv_w_o[0].astype(BF16), wf=wf, wn=wn, cs2=cs2, nyq=nyq,
        ffn_w_in=ffn_w_in.astype(BF16), ffn_conv_w=ffn_conv_w, ffn_conv_b=ffn_conv_b.reshape(-1, 1, D_FF),
        ffn_w_out=ffn_w_out.astype(BF16),
        E=jnp.asarray(e, BF16), ET=jnp.asarray(np.concatenate([e.T, e.T]), BF16),
        tri=jnp.asarray(np.stack([np.tile(tri_f, (1, 2)), np.tile(tri_f.T, (1, 2))]), BF16),
    )


def _trunk(x, p):
    B, T, D = x.shape
    tr = T + N_META
    tp = -(-tr // CHUNK) * CHUNK
    tb, tm, tf = _tiles(tp)
    meta = jnp.broadcast_to(p['meta'].astype(x.dtype)[None], (B, N_META, D))
    h = jnp.concatenate([meta, x, jnp.zeros((B, tp - tr, D), x.dtype)], axis=1)

    r, k, v, kk, g, a, lw = _rwkv_proj(h, p, tb, tr)
    o_f, o_b = _wkv(r, k, v, kk, a, lw, p['k_a'], p['tri'])
    h = _rwkv_out(o_f, o_b, r, k, v, g, a, h, p, tb)
    h, y, ynyq = _ffn(h, p['norm_ffn0'], p['ffn_w_in'][0], p['ffn_conv_w'][0], p['ffn_conv_b'][0],
                      p['ffn_w_out'][0], p['norm_mix1'], p['cs2'], p['nyq'], tf, tr, out_rows=tp, tail='fnet_c')
    ct, st = _time_dft_matrices(tr, tp)
    h = _fnet_t(ct, st, y, ynyq, h, p['wf'], p['wn'], tm)
    return _ffn(h, p['norm_ffn1'], p['ffn_w_in'][1], p['ffn_conv_w'][1], p['ffn_conv_b'][1], p['ffn_w_out'][1],
                p['norm_final'], p['cs2'], p['nyq'], tf, tr, out_rows=T, tail='final_norm')[0]


def kernel(x_prompt, x_sample, meta_tokens, norm_mix, norm_ffn, norm_final, rwkv_mu, rwkv_w_rkv, rwkv_w0, rwkv_w1, rwkv_w2, rwkv_a0, rwkv_a1, rwkv_a2, rwkv_g1, rwkv_g2, rwkv_k_k, rwkv_k_a, rwkv_r_k, rwkv_gn_w, rwkv_gn_b, rwkv_w_o, fnet_w_o, ffn_w_in, ffn_conv_w, ffn_conv_b, ffn_w_out):
    p = _prepare(meta_tokens, norm_mix, norm_ffn, norm_final, rwkv_mu, rwkv_w_rkv, rwkv_w0, rwkv_w1, rwkv_w2,
                 rwkv_a0, rwkv_a1, rwkv_a2, rwkv_g1, rwkv_g2, rwkv_k_k, rwkv_k_a, rwkv_r_k, rwkv_gn_w,
                 rwkv_gn_b, rwkv_w_o, fnet_w_o, ffn_w_in, ffn_conv_w, ffn_conv_b, ffn_w_out)
    return (_trunk(x_prompt, p), _trunk(x_sample, p))
```

```python
import functools
import math

import jax
import jax.numpy as jnp
import numpy as np
from jax import lax
from jax.experimental import pallas as pl
from jax.experimental.pallas import tpu as pltpu

D_MODEL = 1024
N_META = 16
HEAD_SIZE = 64
N_HEADS = D_MODEL // HEAD_SIZE
GATE_LORA_PAD = 256
LORA2 = 128
D_FF = 2816
FNET_GROUPS = 8
FNET_GROUP_DIM = D_MODEL // FNET_GROUPS
HALF_D = D_MODEL // 2
RMS_EPS = 1e-6
GN_EPS = 64e-5
LOG2_E = math.log2(math.e)

CHUNK = 64
PAIR = 2 * HEAD_SIZE
N_PAIRS = N_HEADS // 2
HALO = 8
FF_CHUNK = 256
META_SHIFT = N_META
NEXT_ROWS = 32
WKV_SUBCHUNK_CAP = 5
VMEM_LIMIT_BYTES = 56 * 1024 * 1024

BF16 = jnp.bfloat16
F32 = jnp.float32


def _dot(a, b):
    return jnp.dot(a, b, preferred_element_type=F32)


def _dot_nt(a, b):
    return lax.dot_general(a, b, (((1,), (1,)), ((), ())), preferred_element_type=F32)


def _dot_tn(a, b):
    return lax.dot_general(a, b, (((0,), (0,)), ((), ())), preferred_element_type=F32)


def _rmsnorm(x, gain):
    ms = jnp.mean(x * x, axis=-1, keepdims=True)
    return x * lax.rsqrt(ms + RMS_EPS) * gain


def _split_bf16(x):
    hi = x.astype(BF16)
    lo = (x - hi.astype(F32)).astype(BF16)
    return hi, lo


def _head_sum(x, e_ref):
    return _dot(x.astype(BF16), e_ref[...])


def _head_bcast(s, et2_ref):
    hi, lo = _split_bf16(s)
    return _dot(jnp.concatenate([hi, lo], axis=1), et2_ref[...])


def _normed_window(h_ref, hp_ref, hn_ref, gain, i, nblk, tb, tr):
    row0 = i * tb
    rows = row0 + lax.broadcasted_iota(jnp.int32, (tb, 1), 0)
    valid = rows < tr
    x = jnp.where(valid, _rmsnorm(h_ref[...], gain), 0.0)
    halo = lax.broadcasted_iota(jnp.int32, (HALO, 1), 0)
    rp = row0 - HALO + halo
    xp = jnp.where(jnp.logical_and(rp >= 0, rp < tr), _rmsnorm(hp_ref[...], gain), 0.0)
    xn = jnp.where(jnp.logical_and(i < nblk - 1, row0 + tb + halo < tr), _rmsnorm(hn_ref[...], gain), 0.0)
    return x, xp, xn, valid


def _shift_rows(x, xp, xn, tb):
    ridx = lax.broadcasted_iota(jnp.int32, (tb, 1), 0)
    x_prev = jnp.where(ridx == 0, xp, pltpu.roll(x, 1, 0))
    x_next = jnp.where(ridx == tb - 1, xn, pltpu.roll(x, tb - 1, 0))
    return x_prev, x_next


def _proj_kernel(h_ref, hp_ref, hn_ref, gain_ref, mu_ref, wrkv_ref, lora1_ref, w2_ref, w0_ref,
                 a2_ref, a0_ref, g2_ref, kk_ref, e_ref, et_ref,
                 rkvk_out, g_out, a_out, lw_out, *, tb, tr, nblk):
    i = pl.program_id(1)
    D = D_MODEL
    x, xp, xn, valid = _normed_window(h_ref, hp_ref, hn_ref, gain_ref[...], i, nblk, tb, tr)
    x_prev, x_next = _shift_rows(x, xp[HALO - 1:HALO, :], xn[0:1, :], tb)
    xx = jnp.where(valid, 0.5 * (x_prev + x_next) - x, 0.0)
    mu = mu_ref[...]

    half = tb // 2
    parts = [dict(rows=slice(q * half, (q + 1) * half)) for q in range(2)]

    def mix(pt, j):
        rows = pt['rows']
        return (x[rows] + xx[rows] * mu[j:j + 1, :]).astype(BF16)

    for pt in parts:
        pt['k'] = _dot(mix(pt, 2), wrkv_ref[1])
    for pt in parts:
        pt['tw'] = _dot(mix(pt, 1), lora1_ref[:, :LORA2])
        pt['ta'] = _dot(mix(pt, 4), lora1_ref[:, LORA2:2 * LORA2])
        pt['tg'] = _dot(mix(pt, 5), lora1_ref[:, 2 * LORA2:])
    for pt in parts:
        pt['r'] = _dot(mix(pt, 0), wrkv_ref[0])
    for pt in parts:
        pt['kkr'] = pt['k'] * kk_ref[...]
        pt['ssq'] = _head_sum(pt['kkr'] * pt['kkr'], e_ref)
    for pt in parts:
        pt['v'] = _dot(mix(pt, 3), wrkv_ref[2])
    for pt in parts:
        pt['w_lin'] = w0_ref[...] + _dot(jnp.tanh(pt['tw']).astype(BF16), w2_ref[...])
        pt['a_lin'] = a0_ref[...] + _dot(pt['ta'].astype(BF16), a2_ref[...])
        pt['gate'] = _dot(jax.nn.sigmoid(pt['tg']).astype(BF16), g2_ref[...])
    for pt in parts:
        pt['inv'] = _head_bcast(1.0 / jnp.maximum(jnp.sqrt(pt['ssq']), 1e-12), et_ref)
    for pt in parts:
        rows = pt['rows']
        rkvk_out[rows, 0:D] = pt['r'].astype(BF16)
        rkvk_out[rows, D:2 * D] = pt['k'].astype(BF16)
        rkvk_out[rows, 2 * D:3 * D] = pt['v'].astype(BF16)
        lw_out[rows, :] = (-math.exp(-0.5) * LOG2_E) * jax.nn.sigmoid(pt['w_lin'])
        a_out[rows, :] = jax.nn.sigmoid(pt['a_lin']).astype(BF16)
        g_out[rows, :] = pt['gate'].astype(BF16)
        rkvk_out[rows, 3 * D:4 * D] = (pt['kkr'] * pt['inv']).astype(BF16)


def _row_block_specs(tb, nhalo_blocks):
    per = tb // HALO
    main = pl.BlockSpec((None, tb, D_MODEL), lambda b, i, *_: (b, i, 0))
    prev = pl.BlockSpec((None, HALO, D_MODEL), lambda b, i, *_: (b, jnp.maximum(i * per - 1, 0), 0))
    nxt = pl.BlockSpec((None, HALO, D_MODEL),
                       lambda b, i, *_: (b, jnp.minimum((i + 1) * per, nhalo_blocks - 1), 0))
    return main, prev, nxt


def _const_spec(shape):
    nd = len(shape)
    return pl.BlockSpec(shape, lambda *_: (0,) * nd)


def _rwkv_proj(h, p, tb, tr):
    B, Tp, D = h.shape
    nblk = Tp // tb
    main, prev, nxt = _row_block_specs(tb, Tp // HALO)
    row = lambda w: pl.BlockSpec((None, tb, w), lambda b, i: (b, i, 0))
    consts = [p['norm_mix0'], p['mu'], p['w_rkv'], p['lora1'], p['w2'], p['w0'], p['a2'], p['a0'],
              p['g2'], p['k_k'], p['E'], p['ET']]
    out_shape = [jax.ShapeDtypeStruct((B, Tp, 4 * D), BF16), jax.ShapeDtypeStruct((B, Tp, D), BF16),
                 jax.ShapeDtypeStruct((B, Tp, 2 * D), BF16), jax.ShapeDtypeStruct((B, Tp, 2 * D), F32)]
    return pl.pallas_call(
        functools.partial(_proj_kernel, tb=tb, tr=tr, nblk=nblk),
        grid=(B, nblk),
        in_specs=[main, prev, nxt] + [_const_spec(c.shape) for c in consts],
        out_specs=[row(4 * D), row(D), row(2 * D), row(2 * D)],
        out_shape=out_shape,
        compiler_params=pltpu.CompilerParams(
            dimension_semantics=("parallel", "arbitrary"), vmem_limit_bytes=VMEM_LIMIT_BYTES),
        name="rwkv_proj",
    )(h, h, h, *consts)


def _wkv_kernel(rkvkf_ref, af_ref, lwf_ref, rkvkb_ref, ab_ref, lwb_ref,
                ka_ref, tri_ref, of_ref, ob_ref, state_ref, *, nsub):
    c = pl.program_id(1)
    D = D_MODEL
    rf_ref, kf_ref, vf_ref, kkf_ref = (rkvkf_ref.at[:, j * D:(j + 1) * D] for j in range(4))
    rb_ref, kb_ref, vb_ref, kkb_ref = (rkvkb_ref.at[:, j * D:(j + 1) * D] for j in range(4))

    @pl.when(c == 0)
    def _():
        state_ref[...] = jnp.zeros_like(state_ref)

    for sub in range(nsub):
        chains = _wkv_prep(rf_ref, kf_ref, vf_ref, kkf_ref, af_ref, lwf_ref, rb_ref, kb_ref, vb_ref, kkb_ref,
                           ab_ref, lwb_ref, ka_ref, tri_ref, of_ref, ob_ref,
                           pl.ds(sub * CHUNK, CHUNK), pl.ds((nsub - 1 - sub) * CHUNK, CHUNK))
        for ch in chains:
            _wkv_scores(ch)
        _wkv_solve(chains, state_ref)


def _wkv_scores(ch):
    ch['a_all'] = _dot(ch['ar'], ch['bk'])


def _wkv_prep(rf_ref, kf_ref, vf_ref, kkf_ref, af_ref, lwf_ref, rb_ref, kb_ref, vb_ref, kkb_ref, ab_ref, lwb_ref,
              ka_ref, tri_ref, of_ref, ob_ref, rows_f, rows_b):
    L = CHUNK
    t_idx = lax.broadcasted_iota(jnp.int32, (L, PAIR), 0)
    lane = lax.broadcasted_iota(jnp.int32, (L, PAIR), 1)
    s_idx = lane % HEAD_SIZE
    head0 = lax.broadcasted_iota(jnp.int32, (1, PAIR), 1) < HEAD_SIZE
    ka = ka_ref[...]
    one_minus_ka = 1.0 - ka

    dirs = (
        (0, rows_f, rf_ref, kf_ref, vf_ref, kkf_ref, af_ref, lwf_ref, of_ref, s_idx < t_idx, s_idx <= t_idx, L - 1),
        (1, rows_b, rb_ref, kb_ref, vb_ref, kkb_ref, ab_ref, lwb_ref, ob_ref, s_idx > t_idx, s_idx >= t_idx, 0),
    )
    chains = []
    for d, rows, r_ref, k_ref, v_ref, kk_ref, a_ref, lw_ref, o_ref, strict, incl, last in dirs:
        lw = lw_ref[rows, :]
        hi, lo = _split_bf16(lw)
        cum = _dot(tri_ref[d], jnp.concatenate([hi, lo], axis=0))
        cum_last = cum[last:last + 1, :]
        g_incl = jnp.exp2(cum)
        g_excl = jnp.exp2(cum - lw)
        g_inv = jnp.exp2(-cum)
        g_tail = jnp.exp2(cum_last - cum)
        g_last = jnp.exp2(cum_last)

        kk = kk_ref[rows, :].astype(F32)
        a = a_ref[rows, :].astype(F32)
        kdir = k_ref[rows, :].astype(F32) * (a * ka + one_minus_ka)
        b = kk * a
        a_t = (-kk * g_excl).astype(BF16)
        r_t = (r_ref[rows, :].astype(F32) * g_incl).astype(BF16)
        b_t = b * g_inv
        k_t = kdir * g_inv
        b_h = (b * g_tail).astype(BF16)
        k_h = (kdir * g_tail).astype(BF16)
        v = v_ref[rows, :]

        for p in range(N_PAIRS):
            sl = slice(p * PAIR, (p + 1) * PAIR)
            bp, kp, vp = b_t[:, sl], k_t[:, sl], v[:, sl]
            chains.append(dict(
                d=d, p=p, sl=sl, rows=rows, o_ref=o_ref, strict=strict, incl=incl, vp=vp,
                ar=jnp.concatenate([a_t[:, sl], r_t[:, sl]], axis=0),
                bk=jnp.concatenate([jnp.where(head0, bp, 0.0), jnp.where(head0, 0.0, bp),
                                    jnp.where(head0, kp, 0.0), jnp.where(head0, 0.0, kp)],
                                   axis=0).T.astype(BF16),
                v_bd=jnp.concatenate([jnp.where(head0, vp, 0), jnp.where(head0, 0, vp)], axis=0),
                bk_h=jnp.concatenate([b_h[:, sl], k_h[:, sl]], axis=0),
                g_last=g_last[:, sl]))
    return chains


def _wkv_solve(chains, state_ref):
    L = CHUNK
    head0 = lax.broadcasted_iota(jnp.int32, (1, PAIR), 1) < HEAD_SIZE
    steps = int(math.log2(L))
    for ch in chains:
        ch['s0'] = state_ref[ch['d'], ch['p']]
        s0b = ch['s0'].astype(BF16)
        s0b = jnp.concatenate([jnp.where(head0, s0b[:HEAD_SIZE], 0), jnp.where(head0, 0, s0b[HEAD_SIZE:])], axis=0)
        ch['ah'] = _dot_nt(ch['ar'], s0b)
    for ch in chains:
        a_all, strict, incl = ch['a_all'], ch['strict'], ch['incl']
        ch['pk'] = jnp.where(strict, a_all[:L, :PAIR], 0.0)
        a_ak = jnp.where(strict, a_all[:L, PAIR:], 0.0).astype(BF16)
        ch['a_rbk'] = jnp.concatenate([jnp.where(incl, a_all[L:, :PAIR], 0.0),
                                       jnp.where(incl, a_all[L:, PAIR:], 0.0)], axis=1).astype(BF16)
        ch['w'] = ch['ah'][:L] + _dot(a_ak, ch['v_bd'])
    for step in range(steps):
        for ch in chains:
            pkb = ch['pk'].astype(BF16)
            wb = ch['w'].astype(BF16)
            if step < steps - 1:
                rhs = jnp.concatenate(
                    [jnp.concatenate([jnp.where(head0, pkb, 0), jnp.where(head0, wb, 0)], axis=1),
                     jnp.concatenate([jnp.where(head0, 0, pkb), jnp.where(head0, 0, wb)], axis=1)], axis=0)
                out = _dot(pkb, rhs)
                ch['pk'] = out[:, :PAIR]
                ch['w'] = ch['w'] + out[:, PAIR:]
            else:
                rhs = jnp.concatenate([jnp.where(head0, wb, 0), jnp.where(head0, 0, wb)], axis=0)
                ch['w'] = ch['w'] + _dot(pkb, rhs)
    for ch in chains:
        ub = ch['w'].astype(BF16)
        uv_bd = jnp.concatenate([jnp.where(head0, ub, 0), jnp.where(head0, 0, ub), ch['v_bd']], axis=0)
        ch['o_ref'][ch['rows'], ch['sl']] = (ch['ah'][L:] + _dot(ch['a_rbk'], uv_bd)).astype(ch['o_ref'].dtype)
        upd = _dot_tn(jnp.concatenate([ub, ch['vp']], axis=0), ch['bk_h'])
        state_ref[ch['d'], ch['p']] = ch['s0'] * ch['g_last'] + upd


def _wkv(rkvk, a, lw, k_a, tri):
    B, Tp, D = a.shape[0], a.shape[1], a.shape[2] // 2
    nsub = _largest_divisor(Tp // CHUNK, 1, WKV_SUBCHUNK_CAP)
    rows = nsub * CHUNK
    n = Tp // rows
    fwd = lambda col, w=D: pl.BlockSpec((None, rows, w), lambda b, c: (b, c, col))
    bwd = lambda col, w=D: pl.BlockSpec((None, rows, w), lambda b, c: (b, n - 1 - c, col))
    in_specs = [fwd(0, 4 * D), fwd(0), fwd(0), bwd(0, 4 * D), bwd(1), bwd(1),
                _const_spec(k_a.shape), _const_spec(tri.shape)]
    return pl.pallas_call(
        functools.partial(_wkv_kernel, nsub=nsub),
        grid=(B, n),
        in_specs=in_specs,
        out_specs=[fwd(0), bwd(0)],
        out_shape=[jax.ShapeDtypeStruct((B, Tp, D), BF16)] * 2,
        scratch_shapes=[pltpu.VMEM((2, N_PAIRS, PAIR, PAIR), F32)],
        compiler_params=pltpu.CompilerParams(
            dimension_semantics=("parallel", "arbitrary"), vmem_limit_bytes=VMEM_LIMIT_BYTES),
        name="wkv",
    )(rkvk, a, lw, rkvk, a, lw, k_a, tri)


def _rwkv_out_kernel(of_ref, ob_ref, r_ref, k_ref, v_ref, g_ref, af_ref, ab_ref, h_ref,
                     rk_ref, ka_ref, gnw_ref, gnb_ref, wo_ref, e_ref, et_ref, out_ref, *, tb):
    inv_n = 1.0 / HEAD_SIZE
    half = tb // 2
    parts = [dict(rows=slice(q * half, (q + 1) * half)) for q in range(2)]
    for pt in parts:
        rows = pt['rows']
        pt['o'] = of_ref[rows, :].astype(F32) + ob_ref[rows, :].astype(F32)
        a_sum = af_ref[rows, :].astype(F32) + ab_ref[rows, :].astype(F32)
        rk = (r_ref[rows, :].astype(F32) * rk_ref[...] * k_ref[rows, :].astype(F32)
              * (2.0 + (a_sum - 2.0) * ka_ref[...]))
        pt['s_o'] = _head_sum(pt['o'], e_ref)
        pt['s_rk'] = _head_sum(rk, e_ref)
    for pt in parts:
        pt['dlt'] = pt['o'] - _head_bcast(pt['s_o'] * inv_n, et_ref)
        pt['bonus'] = _head_bcast(pt['s_rk'], et_ref)
    for pt in parts:
        pt['s_v'] = _head_sum(pt['dlt'] * pt['dlt'], e_ref)
    for pt in parts:
        rows = pt['rows']
        var = _head_bcast(pt['s_v'] * inv_n, et_ref)
        on = pt['dlt'] * lax.rsqrt(var + GN_EPS) * gnw_ref[...] + gnb_ref[...]
        on = on + pt['bonus'] * v_ref[rows, :].astype(F32)
        y = (on * g_ref[rows, :].astype(F32)).astype(BF16)
        out_ref[rows, :] = h_ref[rows, :] + _dot(y, wo_ref[...])


def _rwkv_out(o_f, o_b, rkvk, g, a, h, p, tb):
    B, Tp, D = h.shape
    row = lambda col=0: pl.BlockSpec((None, tb, D), lambda b, i: (b, i, col))
    consts = [p['r_k'], p['k_a'], p['gn_w'], p['gn_b'], p['w_o'], p['E'], p['ET']]
    return pl.pallas_call(
        functools.partial(_rwkv_out_kernel, tb=tb),
        grid=(B, Tp // tb),
        in_specs=([row(), row(), row(0), row(1), row(2), row(), row(0), row(1), row()]
                  + [_const_spec(c.shape) for c in consts]),
        out_specs=row(),
        out_shape=jax.ShapeDtypeStruct((B, Tp, D), F32),
        compiler_params=pltpu.CompilerParams(
            dimension_semantics=("parallel", "parallel"), vmem_limit_bytes=VMEM_LIMIT_BYTES),
        name="rwkv_out",
    )(o_f, o_b, rkvk, rkvk, rkvk, g, a, a, h, *consts)


def _channel_dft(x, cs2_ref, nyq_ref, y_ref, ynyq_ref):
    cs2 = cs2_ref[...]
    for q in range(FNET_GROUPS // 2):
        y = _dot(x[:, q * 2 * FNET_GROUP_DIM:(q + 1) * 2 * FNET_GROUP_DIM], cs2).astype(BF16)
        y_ref[:, q * FNET_GROUP_DIM:(q + 1) * FNET_GROUP_DIM] = y[:, :FNET_GROUP_DIM]
        y_ref[:, HALF_D + q * FNET_GROUP_DIM:HALF_D + (q + 1) * FNET_GROUP_DIM] = y[:, FNET_GROUP_DIM:]
    ynyq_ref[...] = _dot(x, nyq_ref[...]).astype(BF16)


def _ffn_kernel(h_ref, hp_ref, hn_ref, gain_ref, win_ref, cw_ref, cb_ref, wout_ref, gtail_ref, cs2_ref, nyq_ref,
                out_ref, *y_refs, tb, tr, shift, tail):
    i = pl.program_id(1)
    n_win = tb + 2 * HALO
    if shift == 0:
        hw = jnp.concatenate([hp_ref[...], h_ref[...], hn_ref[0:HALO, :]], axis=0)
        h_res = h_ref[...]
    else:
        hw = jnp.concatenate([h_ref[shift - HALO:, :], hn_ref[0:shift + HALO, :]], axis=0)
        h_res = jnp.concatenate([h_ref[shift:, :], hn_ref[0:shift, :]], axis=0)
    rows = i * tb + (shift - HALO) + lax.broadcasted_iota(jnp.int32, (n_win, 1), 0)
    valid = jnp.logical_and(rows >= 0, rows < tr)
    xw = jnp.where(valid, _rmsnorm(hw, gain_ref[...]), 0.0).astype(BF16)
    x_mid = xw[HALO:HALO + tb]

    def in_dots(c0, width):
        u_act = _dot(xw, win_ref[:, c0:c0 + width])
        u_lin = _dot(x_mid, win_ref[:, D_FF + c0:D_FF + c0 + width])
        return u_act, u_lin

    chunks = [(c0, min(FF_CHUNK, D_FF - c0)) for c0 in range(0, D_FF, FF_CHUNK)]
    acc = None
    nxt = in_dots(*chunks[0])
    for ci, (c0, width) in enumerate(chunks):
        u_act, u_lin = nxt
        if ci + 1 < len(chunks):
            nxt = in_dots(*chunks[ci + 1])
        cw = cw_ref[:, c0:c0 + width]
        c = (pltpu.roll(u_act, 1, 0)[HALO:HALO + tb] * cw[0:1, :]
             + u_act[HALO:HALO + tb] * cw[1:2, :]
             + pltpu.roll(u_act, n_win - 1, 0)[HALO:HALO + tb] * cw[2:3, :]
             + cb_ref[:, c0:c0 + width])
        y = (c * jax.nn.sigmoid(c) * u_lin).astype(BF16)
        part = _dot(y, wout_ref[c0:c0 + width, :])
        acc = part if acc is None else acc + part
    res = h_res + acc
    if tail == 'final_norm':
        res = _rmsnorm(res, gtail_ref[...])
    out_ref[...] = res
    if tail == 'fnet_c':
        out_rows = i * tb + lax.broadcasted_iota(jnp.int32, (tb, 1), 0)
        xn = jnp.where(out_rows < tr, _rmsnorm(res, gtail_ref[...]), 0.0).astype(BF16)
        _channel_dft(xn, cs2_ref, nyq_ref, *y_refs)


def _ffn(h, gain, w_in, conv_w, conv_b, w_out, g_tail, cs2, nyq, tb, tr, out_rows, tail):
    B, Tp, D = h.shape
    shift = 0 if out_rows == Tp else META_SHIFT
    assert tail in ('final_norm', 'fnet_c') and (tail != 'fnet_c' or shift == 0)
    out_specs = [pl.BlockSpec((None, tb, D), lambda b, i: (b, i, 0))]
    out_shape = [jax.ShapeDtypeStruct((B, out_rows, D), F32)]
    if tail == 'fnet_c':
        out_specs += [pl.BlockSpec((None, tb, D), lambda b, i: (b, i, 0)),
                      pl.BlockSpec((None, tb, LANES), lambda b, i: (b, i, 0))]
        out_shape += [jax.ShapeDtypeStruct((B, Tp, D), BF16), jax.ShapeDtypeStruct((B, Tp, LANES), BF16)]
    nblk = pl.cdiv(out_rows, tb)
    per_prev, per_next, n_next = tb // HALO, tb // NEXT_ROWS, Tp // NEXT_ROWS
    resident = lambda shape: pl.BlockSpec(shape, lambda *_: (0,) * len(shape), pipeline_mode=pl.Buffered(1))
    in_specs = [
        pl.BlockSpec((None, tb, D), lambda b, i: (b, i, 0)),
        pl.BlockSpec((None, HALO, D), lambda b, i: (b, jnp.maximum(i * per_prev - 1, 0), 0)),
        pl.BlockSpec((None, NEXT_ROWS, D), lambda b, i: (b, jnp.minimum((i + 1) * per_next, n_next - 1), 0)),
        _const_spec(gain.shape), resident(w_in.shape), _const_spec(conv_w.shape), _const_spec(conv_b.shape),
        resident(w_out.shape), _const_spec(g_tail.shape), _const_spec(cs2.shape), _const_spec(nyq.shape),
    ]
    return pl.pallas_call(
        functools.partial(_ffn_kernel, tb=tb, tr=tr, shift=shift, tail=tail),
        grid=(B, nblk),
        in_specs=in_specs,
        out_specs=out_specs,
        out_shape=out_shape,
        compiler_params=pltpu.CompilerParams(
            dimension_semantics=("parallel", "parallel"), vmem_limit_bytes=VMEM_LIMIT_BYTES),
        name="ffn_" + tail,
    )(h, h, h, gain, w_in, conv_w, conv_b, w_out, g_tail, cs2, nyq)


def _fnet_t_kernel(ct_ref, st_ref, y_ref, ynyq_ref, h_ref, wf_ref, wn_ref, out_ref):
    ct = ct_ref[...]
    zc = _dot(ct, y_ref[:, :HALF_D])
    zs = _dot(st_ref[...], y_ref[:, HALF_D:])
    zn = _dot(ct, ynyq_ref[...])
    z = jnp.concatenate([zc, zs], axis=1).astype(BF16)
    out_ref[...] = h_ref[...] + _dot(z, wf_ref[...]) + _dot(zn.astype(BF16), wn_ref[...])


def _fnet_t(ct, st, y, ynyq, h, wf, wn, tm):
    B, Tp, D = h.shape
    return pl.pallas_call(
        _fnet_t_kernel,
        grid=(B, Tp // tm),
        in_specs=[pl.BlockSpec((tm, Tp), lambda b, m: (m, 0)),
                  pl.BlockSpec((tm, Tp), lambda b, m: (m, 0)),
                  pl.BlockSpec((None, Tp, D), lambda b, m: (b, 0, 0)),
                  pl.BlockSpec((None, Tp, LANES), lambda b, m: (b, 0, 0)),
                  pl.BlockSpec((None, tm, D), lambda b, m: (b, m, 0)),
                  _const_spec(wf.shape), _const_spec(wn.shape)],
        out_specs=pl.BlockSpec((None, tm, D), lambda b, m: (b, m, 0)),
        out_shape=jax.ShapeDtypeStruct((B, Tp, D), F32),
        compiler_params=pltpu.CompilerParams(
            dimension_semantics=("parallel", "parallel"), vmem_limit_bytes=VMEM_LIMIT_BYTES),
        name="fnet_t",
    )(ct, st, y, ynyq, h, wf, wn)


def _dft_table(n, rows, cols):
    j = lax.broadcasted_iota(jnp.int32, (rows, cols), 0)
    k = lax.broadcasted_iota(jnp.int32, (rows, cols), 1)
    ang = ((j * k) % n).astype(F32) * (2.0 * math.pi / n)
    return jnp.cos(ang), jnp.sin(ang)


def _time_dft_matrices(tr, tp):
    nj1 = tp // CHUNK
    k1 = lax.broadcasted_iota(jnp.int32, (nj1, tp), 1)
    j1 = lax.broadcasted_iota(jnp.int32, (nj1, tp), 0)
    ang_a = ((CHUNK * j1 * k1) % tr).astype(F32) * (2.0 * math.pi / tr)
    ca, sa = jnp.cos(ang_a)[:, None, :], jnp.sin(ang_a)[:, None, :]
    cb, sb = _dft_table(tr, CHUNK, tp)
    cb, sb = cb[None], sb[None]
    j = lax.broadcasted_iota(jnp.int32, (tp, tp), 0)
    k = lax.broadcasted_iota(jnp.int32, (tp, tp), 1)
    ok = jnp.logical_and(j < tr, k < tr)
    scale = 1.0 / math.sqrt(tr)
    c = jnp.where(ok, (ca * cb - sa * sb).reshape(tp, tp) * scale, 0.0)
    s = jnp.where(ok, (sa * cb + ca * sb).reshape(tp, tp) * scale, 0.0)
    return c.astype(BF16), s.astype(BF16)


def _channel_dft_tables():
    n, half = FNET_GROUP_DIM, FNET_GROUP_DIM // 2
    c = np.arange(n)[:, None]
    m = np.arange(half)[None, :]
    cos = np.cos(2 * np.pi * ((c * m) % n) / n)
    sin = np.sin(2 * np.pi * ((c * m) % n) / n)
    alt = np.where(np.arange(n) % 2 == 0, 1.0, -1.0)
    sin[:, 0] = alt
    cs2 = np.zeros((2 * n, 2 * n))
    for q in range(2):
        cs2[q * n:(q + 1) * n, q * half:(q + 1) * half] = cos
        cs2[q * n:(q + 1) * n, n + q * half:n + (q + 1) * half] = sin
    nyq = np.zeros((D_MODEL, LANES))
    for g in range(FNET_GROUPS):
        nyq[g * n:(g + 1) * n, g] = alt
    scale = 1.0 / math.sqrt(n)
    return jnp.asarray(cs2 * scale, BF16), jnp.asarray(nyq * scale, BF16)


def _fold_fnet_weight(w_f):
    n, half = FNET_GROUP_DIM, FNET_GROUP_DIM // 2
    w = w_f.reshape(FNET_GROUPS, n, D_MODEL)
    mirror = jnp.pad(jnp.flip(w[:, half + 1:], axis=1), ((0, 0), (1, 0), (0, 0)))
    w_cos = w[:, :half] + mirror
    w_sin = (mirror - w[:, :half]).at[:, 0].set(0.0)
    wf = jnp.concatenate([w_cos.reshape(HALF_D, D_MODEL), w_sin.reshape(HALF_D, D_MODEL)], axis=0)
    wn = jnp.pad(w[:, half], ((0, LANES - FNET_GROUPS), (0, 0)))
    return wf.astype(BF16), wn.astype(BF16)


def _largest_divisor(n, unit, cap):
    best = unit
    for m in range(unit, cap + 1, unit):
        if n % m == 0:
            best = m
    return best


BF16_ROWS = 16
LANES = 128
ROW_BLOCK_CAP = 448
DFT_BLOCK_BYTES_CAP = 7 * 1024 * 1024 // 2
FFN_ROW_BLOCK_CAP = 896


def _tiles(tp):
    tb = _largest_divisor(tp, BF16_ROWS, ROW_BLOCK_CAP)
    tm = _largest_divisor(tp, BF16_ROWS, max(BF16_ROWS, DFT_BLOCK_BYTES_CAP // (2 * tp)))
    tf = _largest_divisor(tp, NEXT_ROWS, FFN_ROW_BLOCK_CAP)
    return tb, tm, tf


def _prepare(meta_tokens, norm_mix, norm_ffn, norm_final, rwkv_mu, rwkv_w_rkv, rwkv_w0, rwkv_w1, rwkv_w2,
             rwkv_a0, rwkv_a1, rwkv_a2, rwkv_g1, rwkv_g2, rwkv_k_k, rwkv_k_a, rwkv_r_k, rwkv_gn_w, rwkv_gn_b,
             rwkv_w_o, fnet_w_o, ffn_w_in, ffn_conv_w, ffn_conv_b, ffn_w_out):
    D = D_MODEL
    row = lambda x: x.reshape(1, -1).astype(F32)

    def lora_pair(w1, w2, w0):
        first = jnp.concatenate([w1[0], w1[1]], axis=1)
        z = jnp.zeros_like(w2[0])
        second = jnp.concatenate([jnp.concatenate([w2[0], z], axis=1),
                                  jnp.concatenate([z, w2[1]], axis=1)], axis=0)
        return first, second.astype(BF16), jnp.concatenate([w0[0], w0[1]]).reshape(1, 2 * D)

    w1, w2, w0 = lora_pair(rwkv_w1[0], rwkv_w2[0], rwkv_w0[0])
    a1, a2, a0 = lora_pair(rwkv_a1[0], rwkv_a2[0], rwkv_a0[0])
    rank = rwkv_g1.shape[-1]
    g1 = jnp.pad(rwkv_g1[0], ((0, 0), (0, GATE_LORA_PAD - rank)))
    g2 = jnp.pad(rwkv_g2[0], ((0, GATE_LORA_PAD - rank), (0, 0))).astype(BF16)
    lora1 = jnp.concatenate([w1, a1, g1], axis=1).astype(BF16)
    head_of_lane = np.arange(D) // HEAD_SIZE
    e = (head_of_lane[:, None] == np.arange(128)[None, :]).astype(np.float32)
    cs2, nyq = _channel_dft_tables()
    wf, wn = _fold_fnet_weight(fnet_w_o[0])
    tri_f = np.tril(np.ones((CHUNK, CHUNK), np.float32))
    return dict(
        meta=meta_tokens,
        norm_mix0=row(norm_mix[0]), norm_mix1=row(norm_mix[1]),
        norm_ffn0=row(norm_ffn[0]), norm_ffn1=row(norm_ffn[1]), norm_final=row(norm_final),
        mu=jnp.pad(rwkv_mu[0], ((0, 2), (0, 0))),
        w_rkv=rwkv_w_rkv[0].astype(BF16), lora1=lora1, w2=w2, w0=w0, a2=a2, a0=a0, g2=g2,
        k_k=row(rwkv_k_k[0]), k_a=row(rwkv_k_a[0]), r_k=row(rwkv_r_k[0]),
        gn_w=row(rwkv_gn_w[0]), gn_b=row(rwkv_gn_b[0]),
        w_o=rwkv_w_o[0].astype(BF16), wf=wf, wn=wn, cs2=cs2, nyq=nyq,
        ffn_w_in=ffn_w_in.astype(BF16), ffn_conv_w=ffn_conv_w, ffn_conv_b=ffn_conv_b.reshape(-1, 1, D_FF),
        ffn_w_out=ffn_w_out.astype(BF16),
        E=jnp.asarray(e, BF16), ET=jnp.asarray(np.concatenate([e.T, e.T]), BF16),
        tri=jnp.asarray(np.stack([np.tile(tri_f, (1, 2)), np.tile(tri_f.T, (1, 2))]), BF16),
    )


def _trunk(x, p):
    B, T, D = x.shape
    tr = T + N_META
    tp = -(-tr // CHUNK) * CHUNK
    tb, tm, tf = _tiles(tp)
    meta = jnp.broadcast_to(p['meta'].astype(x.dtype)[None], (B, N_META, D))
    h = jnp.concatenate([meta, x, jnp.zeros((B, tp - tr, D), x.dtype)], axis=1)

    rkvk, g, a, lw = _rwkv_proj(h, p, tb, tr)
    o_f, o_b = _wkv(rkvk, a, lw, p['k_a'], p['tri'])
    h = _rwkv_out(o_f, o_b, rkvk, g, a, h, p, tb)
    h, y, ynyq = _ffn(h, p['norm_ffn0'], p['ffn_w_in'][0], p['ffn_conv_w'][0], p['ffn_conv_b'][0],
                      p['ffn_w_out'][0], p['norm_mix1'], p['cs2'], p['nyq'], tf, tr, out_rows=tp, tail='fnet_c')
    ct, st = _time_dft_matrices(tr, tp)
    h = _fnet_t(ct, st, y, ynyq, h, p['wf'], p['wn'], tm)
    return _ffn(h, p['norm_ffn1'], p['ffn_w_in'][1], p['ffn_conv_w'][1], p['ffn_conv_b'][1], p['ffn_w_out'][1],
                p['norm_final'], p['cs2'], p['nyq'], tf, tr, out_rows=T, tail='final_norm')[0]


def kernel(x_prompt, x_sample, meta_tokens, norm_mix, norm_ffn, norm_final, rwkv_mu, rwkv_w_rkv, rwkv_w0, rwkv_w1, rwkv_w2, rwkv_a0, rwkv_a1, rwkv_a2, rwkv_g1, rwkv_g2, rwkv_k_k, rwkv_k_a, rwkv_r_k, rwkv_gn_w, rwkv_gn_b, rwkv_w_o, fnet_w_o, ffn_w_in, ffn_conv_w, ffn_conv_b, ffn_w_out):
    p = _prepare(meta_tokens, norm_mix, norm_ffn, norm_final, rwkv_mu, rwkv_w_rkv, rwkv_w0, rwkv_w1, rwkv_w2,
                 rwkv_a0, rwkv_a1, rwkv_a2, rwkv_g1, rwkv_g2, rwkv_k_k, rwkv_k_a, rwkv_r_k, rwkv_gn_w,
                 rwkv_gn_b, rwkv_w_o, fnet_w_o, ffn_w_in, ffn_conv_w, ffn_conv_b, ffn_w_out)
    return (_trunk(x_prompt, p), _trunk(x_sample, p))
```

```python
import functools
import math

import jax
import jax.numpy as jnp
import numpy as np
from jax import lax
from jax.experimental import pallas as pl
from jax.experimental.pallas import tpu as pltpu

D_MODEL = 1024
N_META = 16
HEAD_SIZE = 64
N_HEADS = D_MODEL // HEAD_SIZE
GATE_LORA_PAD = 256
LORA2 = 128
D_FF = 2816
FNET_GROUPS = 8
FNET_GROUP_DIM = D_MODEL // FNET_GROUPS
HALF_D = D_MODEL // 2
RMS_EPS = 1e-6
GN_EPS = 64e-5
LOG2_E = math.log2(math.e)

CHUNK = 64
PAIR = 2 * HEAD_SIZE
N_PAIRS = N_HEADS // 2
HALO = 8
FF_CHUNK = 256
META_SHIFT = N_META
NEXT_ROWS = 32
WKV_SUBCHUNK_CAP = 5
VMEM_LIMIT_BYTES = 56 * 1024 * 1024

BF16 = jnp.bfloat16
F32 = jnp.float32


def _dot(a, b):
    return jnp.dot(a, b, preferred_element_type=F32)


def _dot_nt(a, b):
    return lax.dot_general(a, b, (((1,), (1,)), ((), ())), preferred_element_type=F32)


def _dot_tn(a, b):
    return lax.dot_general(a, b, (((0,), (0,)), ((), ())), preferred_element_type=F32)


def _rmsnorm(x, gain):
    ms = jnp.mean(x * x, axis=-1, keepdims=True)
    return x * lax.rsqrt(ms + RMS_EPS) * gain


def _split_bf16(x):
    hi = x.astype(BF16)
    lo = (x - hi.astype(F32)).astype(BF16)
    return hi, lo


def _head_sum(x, e_ref):
    return _dot(x.astype(BF16), e_ref[...])


def _head_bcast(s, et2_ref):
    hi, lo = _split_bf16(s)
    return _dot(jnp.concatenate([hi, lo], axis=1), et2_ref[...])


def _normed_window(h_ref, hp_ref, hn_ref, gain, i, nblk, tb, tr):
    row0 = i * tb
    rows = row0 + lax.broadcasted_iota(jnp.int32, (tb, 1), 0)
    valid = rows < tr
    x = jnp.where(valid, _rmsnorm(h_ref[...], gain), 0.0)
    xp = _rmsnorm(hp_ref[...], gain)[HALO - 1:HALO, :]
    xp = jnp.where(i > 0, xp, 0.0)
    xn = _rmsnorm(hn_ref[...], gain)[0:1, :]
    xn = jnp.where(jnp.logical_and(i < nblk - 1, row0 + tb < tr), xn, 0.0)
    return x, xp, xn, valid


def _shift_rows(x, xp, xn, tb):
    ridx = lax.broadcasted_iota(jnp.int32, (tb, 1), 0)
    x_prev = jnp.where(ridx == 0, xp, pltpu.roll(x, 1, 0))
    x_next = jnp.where(ridx == tb - 1, xn, pltpu.roll(x, tb - 1, 0))
    return x_prev, x_next


def _proj_kernel(h_ref, hp_ref, hn_ref, gain_ref, mu_ref, wrkv_ref, lora1_ref, w2_ref, w0_ref,
                 a2_ref, a0_ref, g2_ref, kk_ref, rk_ref, ka_ref, e_ref, et_ref,
                 r_out, k_out, v_out, kk_out, g_out, a_out, lw_out, srk_out, *, tb, tr, nblk):
    i = pl.program_id(1)
    x, xp, xn, valid = _normed_window(h_ref, hp_ref, hn_ref, gain_ref[...], i, nblk, tb, tr)
    x_prev, x_next = _shift_rows(x, xp, xn, tb)
    xx = jnp.where(valid, 0.5 * (x_prev + x_next) - x, 0.0)
    mu = mu_ref[...]

    def mix(j):
        return (x + xx * mu[j:j + 1, :]).astype(BF16)

    k = _dot(mix(2), wrkv_ref[1])
    tw = _dot(mix(1), lora1_ref[:, :LORA2])
    ta = _dot(mix(4), lora1_ref[:, LORA2:2 * LORA2])
    tg = _dot(mix(5), lora1_ref[:, 2 * LORA2:])
    r = _dot(mix(0), wrkv_ref[0])
    kkr = k * kk_ref[...]
    ssq = _head_sum(kkr * kkr, e_ref)
    v = _dot(mix(3), wrkv_ref[2])
    w_lin = w0_ref[...] + _dot(jnp.tanh(tw).astype(BF16), w2_ref[...])
    a_lin = a0_ref[...] + _dot(ta.astype(BF16), a2_ref[...])
    gate = _dot(jax.nn.sigmoid(tg).astype(BF16), g2_ref[...])
    inv = _head_bcast(1.0 / jnp.maximum(jnp.sqrt(ssq), 1e-12), et_ref)

    k_out[...] = k.astype(BF16)
    r_out[...] = r.astype(BF16)
    v_out[...] = v.astype(BF16)
    lw_out[...] = (-math.exp(-0.5) * LOG2_E) * jax.nn.sigmoid(w_lin)
    a_sig = jax.nn.sigmoid(a_lin)
    a_out[...] = a_sig.astype(BF16)
    g_out[...] = gate.astype(BF16)
    kk_out[...] = (kkr * inv).astype(BF16)
    a_sum = a_sig[:, :D_MODEL] + a_sig[:, D_MODEL:]
    srk_out[...] = _head_sum(r * rk_ref[...] * k * (2.0 + (a_sum - 2.0) * ka_ref[...]), e_ref)


def _row_block_specs(tb, nhalo_blocks):
    per = tb // HALO
    main = pl.BlockSpec((None, tb, D_MODEL), lambda b, i, *_: (b, i, 0))
    prev = pl.BlockSpec((None, HALO, D_MODEL), lambda b, i, *_: (b, jnp.maximum(i * per - 1, 0), 0))
    nxt = pl.BlockSpec((None, HALO, D_MODEL),
                       lambda b, i, *_: (b, jnp.minimum((i + 1) * per, nhalo_blocks - 1), 0))
    return main, prev, nxt


def _const_spec(shape):
    nd = len(shape)
    return pl.BlockSpec(shape, lambda *_: (0,) * nd)


def _rwkv_proj(h, p, tb, tr):
    B, Tp, D = h.shape
    nblk = Tp // tb
    main, prev, nxt = _row_block_specs(tb, Tp // HALO)
    row = lambda w: pl.BlockSpec((None, tb, w), lambda b, i: (b, i, 0))
    consts = [p['norm_mix0'], p['mu'], p['w_rkv'], p['lora1'], p['w2'], p['w0'], p['a2'], p['a0'],
              p['g2'], p['k_k'], p['r_k'], p['k_a'], p['E'], p['ET']]
    out_shape = [jax.ShapeDtypeStruct((B, Tp, D), BF16)] * 5 + [
        jax.ShapeDtypeStruct((B, Tp, 2 * D), BF16), jax.ShapeDtypeStruct((B, Tp, 2 * D), F32),
        jax.ShapeDtypeStruct((B, Tp, LANES), F32)]
    return pl.pallas_call(
        functools.partial(_proj_kernel, tb=tb, tr=tr, nblk=nblk),
        grid=(B, nblk),
        in_specs=[main, prev, nxt] + [_const_spec(c.shape) for c in consts],
        out_specs=[row(D)] * 5 + [row(2 * D), row(2 * D), row(LANES)],
        out_shape=out_shape,
        compiler_params=pltpu.CompilerParams(
            dimension_semantics=("parallel", "arbitrary"), vmem_limit_bytes=VMEM_LIMIT_BYTES),
        name="rwkv_proj",
    )(h, h, h, *consts)


def _wkv_kernel(rf_ref, kf_ref, vf_ref, kkf_ref, af_ref, lwf_ref,
                rb_ref, kb_ref, vb_ref, kkb_ref, ab_ref, lwb_ref,
                ka_ref, tri_ref, of_ref, ob_ref, state_ref, *, nsub):
    c = pl.program_id(1)

    @pl.when(c == 0)
    def _():
        state_ref[...] = jnp.zeros_like(state_ref)

    for sub in range(nsub):
        chains = _wkv_prep(rf_ref, kf_ref, vf_ref, kkf_ref, af_ref, lwf_ref, rb_ref, kb_ref, vb_ref, kkb_ref,
                           ab_ref, lwb_ref, ka_ref, tri_ref, of_ref, ob_ref,
                           pl.ds(sub * CHUNK, CHUNK), pl.ds((nsub - 1 - sub) * CHUNK, CHUNK))
        for ch in chains:
            _wkv_scores(ch)
        _wkv_solve(chains, state_ref)


def _wkv_scores(ch):
    ch['a_all'] = _dot_nt(ch['ar'], ch['bk'])


def _wkv_prep(rf_ref, kf_ref, vf_ref, kkf_ref, af_ref, lwf_ref, rb_ref, kb_ref, vb_ref, kkb_ref, ab_ref, lwb_ref,
              ka_ref, tri_ref, of_ref, ob_ref, rows_f, rows_b):
    L = CHUNK
    t_idx = lax.broadcasted_iota(jnp.int32, (L, PAIR), 0)
    lane = lax.broadcasted_iota(jnp.int32, (L, PAIR), 1)
    s_idx = lane % HEAD_SIZE
    head0 = lax.broadcasted_iota(jnp.int32, (1, PAIR), 1) < HEAD_SIZE
    ka = ka_ref[...]
    one_minus_ka = 1.0 - ka

    dirs = (
        (0, rows_f, rf_ref, kf_ref, vf_ref, kkf_ref, af_ref, lwf_ref, of_ref, s_idx < t_idx, s_idx <= t_idx, L - 1),
        (1, rows_b, rb_ref, kb_ref, vb_ref, kkb_ref, ab_ref, lwb_ref, ob_ref, s_idx > t_idx, s_idx >= t_idx, 0),
    )
    chains = []
    for d, rows, r_ref, k_ref, v_ref, kk_ref, a_ref, lw_ref, o_ref, strict, incl, last in dirs:
        lw = lw_ref[rows, :]
        hi, lo = _split_bf16(lw)
        cum = _dot(tri_ref[d], jnp.concatenate([hi, lo], axis=0))
        cum_last = cum[last:last + 1, :]
        g_incl = jnp.exp2(cum)
        g_excl = jnp.exp2(cum - lw)
        g_inv = jnp.exp2(-cum)
        g_tail = jnp.exp2(cum_last - cum)
        g_last = jnp.exp2(cum_last)

        kk = kk_ref[rows, :].astype(F32)
        a = a_ref[rows, :].astype(F32)
        kdir = k_ref[rows, :].astype(F32) * (a * ka + one_minus_ka)
        b = kk * a
        a_t = (-kk * g_excl).astype(BF16)
        r_t = (r_ref[rows, :].astype(F32) * g_incl).astype(BF16)
        b_t = b * g_inv
        k_t = kdir * g_inv
        b_h = (b * g_tail).astype(BF16)
        k_h = (kdir * g_tail).astype(BF16)
        v = v_ref[rows, :]

        for p in range(N_PAIRS):
            sl = slice(p * PAIR, (p + 1) * PAIR)
            bp, kp, vp = b_t[:, sl], k_t[:, sl], v[:, sl]
            chains.append(dict(
                d=d, p=p, sl=sl, rows=rows, o_ref=o_ref, strict=strict, incl=incl, vp=vp,
                ar=jnp.concatenate([a_t[:, sl], r_t[:, sl]], axis=0),
                bk=jnp.concatenate([jnp.where(head0, bp, 0.0), jnp.where(head0, 0.0, bp),
                                    jnp.where(head0, kp, 0.0), jnp.where(head0, 0.0, kp)],
                                   axis=0).astype(BF16),
                v_bd=jnp.concatenate([jnp.where(head0, vp, 0), jnp.where(head0, 0, vp)], axis=0),
                bk_h=jnp.concatenate([b_h[:, sl], k_h[:, sl]], axis=0),
                g_last=g_last[:, sl]))
    return chains


def _wkv_solve(chains, state_ref):
    L = CHUNK
    head0 = lax.broadcasted_iota(jnp.int32, (1, PAIR), 1) < HEAD_SIZE
    steps = int(math.log2(L))
    for ch in chains:
        ch['s0'] = state_ref[ch['d'], ch['p']]
        s0b = ch['s0'].astype(BF16)
        s0b = jnp.concatenate([jnp.where(head0, s0b[:HEAD_SIZE], 0), jnp.where(head0, 0, s0b[HEAD_SIZE:])], axis=0)
        ch['ah'] = _dot_nt(ch['ar'], s0b)
    for ch in chains:
        a_all, strict, incl = ch['a_all'], ch['strict'], ch['incl']
        ch['pk'] = jnp.where(strict, a_all[:L, :PAIR], 0.0)
        a_ak = jnp.where(strict, a_all[:L, PAIR:], 0.0).astype(BF16)
        ch['a_rbk'] = jnp.concatenate([jnp.where(incl, a_all[L:, :PAIR], 0.0),
                                       jnp.where(incl, a_all[L:, PAIR:], 0.0)], axis=1).astype(BF16)
        ch['w'] = ch['ah'][:L] + _dot(a_ak, ch['v_bd'])
    for step in range(steps):
        for ch in chains:
            pkb = ch['pk'].astype(BF16)
            wb = ch['w'].astype(BF16)
            if step < steps - 1:
                rhs = jnp.concatenate(
                    [jnp.concatenate([jnp.where(head0, pkb, 0), jnp.where(head0, wb, 0)], axis=1),
                     jnp.concatenate([jnp.where(head0, 0, pkb), jnp.where(head0, 0, wb)], axis=1)], axis=0)
                out = _dot(pkb, rhs)
                ch['pk'] = out[:, :PAIR]
                ch['w'] = ch['w'] + out[:, PAIR:]
            else:
                rhs = jnp.concatenate([jnp.where(head0, wb, 0), jnp.where(head0, 0, wb)], axis=0)
                ch['w'] = ch['w'] + _dot(pkb, rhs)
    for ch in chains:
        ub = ch['w'].astype(BF16)
        uv_bd = jnp.concatenate([jnp.where(head0, ub, 0), jnp.where(head0, 0, ub), ch['v_bd']], axis=0)
        ch['o_ref'][ch['rows'], ch['sl']] = (ch['ah'][L:] + _dot(ch['a_rbk'], uv_bd)).astype(ch['o_ref'].dtype)
        upd = _dot_tn(jnp.concatenate([ub, ch['vp']], axis=0), ch['bk_h'])
        state_ref[ch['d'], ch['p']] = ch['s0'] * ch['g_last'] + upd


def _wkv(r, k, v, kk, a, lw, k_a, tri):
    B, Tp, D = r.shape
    nsub = _largest_divisor(Tp // CHUNK, 1, WKV_SUBCHUNK_CAP)
    rows = nsub * CHUNK
    n = Tp // rows
    fwd = lambda col: pl.BlockSpec((None, rows, D), lambda b, c: (b, c, col))
    bwd = lambda col: pl.BlockSpec((None, rows, D), lambda b, c: (b, n - 1 - c, col))
    in_specs = ([fwd(0)] * 6 + [bwd(0)] * 4 + [bwd(1), bwd(1)]
                + [_const_spec(k_a.shape), _const_spec(tri.shape)])
    return pl.pallas_call(
        functools.partial(_wkv_kernel, nsub=nsub),
        grid=(B, n),
        in_specs=in_specs,
        out_specs=[fwd(0), bwd(0)],
        out_shape=[jax.ShapeDtypeStruct((B, Tp, D), BF16)] * 2,
        scratch_shapes=[pltpu.VMEM((2, N_PAIRS, PAIR, PAIR), F32)],
        compiler_params=pltpu.CompilerParams(
            dimension_semantics=("parallel", "arbitrary"), vmem_limit_bytes=VMEM_LIMIT_BYTES),
        name="wkv",
    )(r, k, v, kk, a, lw, r, k, v, kk, a, lw, k_a, tri)


def _rwkv_out_kernel(of_ref, ob_ref, v_ref, g_ref, srk_ref, h_ref,
                     gnw_ref, gnb_ref, wo_ref, e_ref, et_ref, out_ref, *, tb):
    inv_n = 1.0 / HEAD_SIZE
    half = tb // 2
    parts = [dict(rows=slice(q * half, (q + 1) * half)) for q in range(2)]
    for pt in parts:
        rows = pt['rows']
        pt['o'] = of_ref[rows, :].astype(F32) + ob_ref[rows, :].astype(F32)
        pt['s_o'] = _head_sum(pt['o'], e_ref)
    for pt in parts:
        pt['dlt'] = pt['o'] - _head_bcast(pt['s_o'] * inv_n, et_ref)
        pt['bonus'] = _head_bcast(srk_ref[pt['rows'], :], et_ref)
    for pt in parts:
        pt['s_v'] = _head_sum(pt['dlt'] * pt['dlt'], e_ref)
    for pt in parts:
        rows = pt['rows']
        var = _head_bcast(pt['s_v'] * inv_n, et_ref)
        on = pt['dlt'] * lax.rsqrt(var + GN_EPS) * gnw_ref[...] + gnb_ref[...]
        on = on + pt['bonus'] * v_ref[rows, :].astype(F32)
        y = (on * g_ref[rows, :].astype(F32)).astype(BF16)
        out_ref[rows, :] = h_ref[rows, :] + _dot(y, wo_ref[...])


def _rwkv_out(o_f, o_b, v, g, srk, h, p, tb):
    B, Tp, D = h.shape
    row = lambda w=D: pl.BlockSpec((None, tb, w), lambda b, i: (b, i, 0))
    consts = [p['gn_w'], p['gn_b'], p['w_o'], p['E'], p['ET']]
    return pl.pallas_call(
        functools.partial(_rwkv_out_kernel, tb=tb),
        grid=(B, Tp // tb),
        in_specs=[row()] * 4 + [row(LANES), row()] + [_const_spec(c.shape) for c in consts],
        out_specs=row(),
        out_shape=jax.ShapeDtypeStruct((B, Tp, D), F32),
        compiler_params=pltpu.CompilerParams(
            dimension_semantics=("parallel", "parallel"), vmem_limit_bytes=VMEM_LIMIT_BYTES),
        name="rwkv_out",
    )(o_f, o_b, v, g, srk, h, *consts)


def _channel_dft(x, cs2_ref, nyq_ref, y_ref, ynyq_ref):
    cs2 = cs2_ref[...]
    for q in range(FNET_GROUPS // 2):
        y = _dot(x[:, q * 2 * FNET_GROUP_DIM:(q + 1) * 2 * FNET_GROUP_DIM], cs2).astype(BF16)
        y_ref[:, q * FNET_GROUP_DIM:(q + 1) * FNET_GROUP_DIM] = y[:, :FNET_GROUP_DIM]
        y_ref[:, HALF_D + q * FNET_GROUP_DIM:HALF_D + (q + 1) * FNET_GROUP_DIM] = y[:, FNET_GROUP_DIM:]
    ynyq_ref[...] = _dot(x, nyq_ref[...]).astype(BF16)


def _ffn_kernel(h_ref, hp_ref, hn_ref, gain_ref, win_ref, cw_ref, cb_ref, wout_ref, gtail_ref, cs2_ref, nyq_ref,
                out_ref, *y_refs, tb, tr, shift, tail):
    i = pl.program_id(1)
    n_win = tb + 2 * HALO
    if shift == 0:
        hw = jnp.concatenate([hp_ref[...], h_ref[...], hn_ref[0:HALO, :]], axis=0)
        h_res = h_ref[...]
    else:
        hw = jnp.concatenate([h_ref[shift - HALO:, :], hn_ref[0:shift + HALO, :]], axis=0)
        h_res = jnp.concatenate([h_ref[shift:, :], hn_ref[0:shift, :]], axis=0)
    rows = i * tb + (shift - HALO) + lax.broadcasted_iota(jnp.int32, (n_win, 1), 0)
    valid = jnp.logical_and(rows >= 0, rows < tr)
    xw = jnp.where(valid, _rmsnorm(hw, gain_ref[...]), 0.0).astype(BF16)
    x_mid = xw[HALO:HALO + tb]

    def in_dots(c0, width):
        u_act = _dot(xw, win_ref[:, c0:c0 + width])
        u_lin = _dot(x_mid, win_ref[:, D_FF + c0:D_FF + c0 + width])
        return u_act, u_lin

    chunks = [(c0, min(FF_CHUNK, D_FF - c0)) for c0 in range(0, D_FF, FF_CHUNK)]
    acc = None
    nxt = in_dots(*chunks[0])
    for ci, (c0, width) in enumerate(chunks):
        u_act, u_lin = nxt
        if ci + 1 < len(chunks):
            nxt = in_dots(*chunks[ci + 1])
        cw = cw_ref[:, c0:c0 + width]
        c = (pltpu.roll(u_act, 1, 0)[HALO:HALO + tb] * cw[0:1, :]
             + u_act[HALO:HALO + tb] * cw[1:2, :]
             + pltpu.roll(u_act, n_win - 1, 0)[HALO:HALO + tb] * cw[2:3, :]
             + cb_ref[:, c0:c0 + width])
        y = (c * jax.nn.sigmoid(c) * u_lin).astype(BF16)
        part = _dot(y, wout_ref[c0:c0 + width, :])
        acc = part if acc is None else acc + part
    res = h_res + acc
    if tail == 'final_norm':
        res = _rmsnorm(res, gtail_ref[...])
    out_ref[...] = res
    if tail == 'fnet_c':
        out_rows = i * tb + lax.broadcasted_iota(jnp.int32, (tb, 1), 0)
        xn = jnp.where(out_rows < tr, _rmsnorm(res, gtail_ref[...]), 0.0).astype(BF16)
        _channel_dft(xn, cs2_ref, nyq_ref, *y_refs)


def _ffn(h, gain, w_in, conv_w, conv_b, w_out, g_tail, cs2, nyq, tb, tr, out_rows, tail):
    B, Tp, D = h.shape
    shift = 0 if out_rows == Tp else META_SHIFT
    assert tail in ('final_norm', 'fnet_c') and (tail != 'fnet_c' or shift == 0)
    out_specs = [pl.BlockSpec((None, tb, D), lambda b, i: (b, i, 0))]
    out_shape = [jax.ShapeDtypeStruct((B, out_rows, D), F32)]
    if tail == 'fnet_c':
        out_specs += [pl.BlockSpec((None, tb, D), lambda b, i: (b, i, 0)),
                      pl.BlockSpec((None, tb, LANES), lambda b, i: (b, i, 0))]
        out_shape += [jax.ShapeDtypeStruct((B, Tp, D), BF16), jax.ShapeDtypeStruct((B, Tp, LANES), BF16)]
    nblk = pl.cdiv(out_rows, tb)
    per_prev, per_next, n_next = tb // HALO, tb // NEXT_ROWS, Tp // NEXT_ROWS
    resident = lambda shape: pl.BlockSpec(shape, lambda *_: (0,) * len(shape), pipeline_mode=pl.Buffered(1))
    in_specs = [
        pl.BlockSpec((None, tb, D), lambda b, i: (b, i, 0)),
        pl.BlockSpec((None, HALO, D), lambda b, i: (b, jnp.maximum(i * per_prev - 1, 0), 0)),
        pl.BlockSpec((None, NEXT_ROWS, D), lambda b, i: (b, jnp.minimum((i + 1) * per_next, n_next - 1), 0)),
        _const_spec(gain.shape), resident(w_in.shape), _const_spec(conv_w.shape), _const_spec(conv_b.shape),
        resident(w_out.shape), _const_spec(g_tail.shape), _const_spec(cs2.shape), _const_spec(nyq.shape),
    ]
    return pl.pallas_call(
        functools.partial(_ffn_kernel, tb=tb, tr=tr, shift=shift, tail=tail),
        grid=(B, nblk),
        in_specs=in_specs,
        out_specs=out_specs,
        out_shape=out_shape,
        compiler_params=pltpu.CompilerParams(
            dimension_semantics=("parallel", "parallel"), vmem_limit_bytes=VMEM_LIMIT_BYTES),
        name="ffn_" + tail,
    )(h, h, h, gain, w_in, conv_w, conv_b, w_out, g_tail, cs2, nyq)


def _fnet_t_kernel(ct_ref, st_ref, y_ref, ynyq_ref, h_ref, wf_ref, wn_ref, out_ref):
    ct = ct_ref[...]
    zc = _dot(ct, y_ref[:, :HALF_D])
    zs = _dot(st_ref[...], y_ref[:, HALF_D:])
    zn = _dot(ct, ynyq_ref[...])
    z = jnp.concatenate([zc, zs], axis=1).astype(BF16)
    out_ref[...] = h_ref[...] + _dot(z, wf_ref[...]) + _dot(zn.astype(BF16), wn_ref[...])


def _fnet_t(ct, st, y, ynyq, h, wf, wn, tm):
    B, Tp, D = h.shape
    return pl.pallas_call(
        _fnet_t_kernel,
        grid=(B, Tp // tm),
        in_specs=[pl.BlockSpec((tm, Tp), lambda b, m: (m, 0)),
                  pl.BlockSpec((tm, Tp), lambda b, m: (m, 0)),
                  pl.BlockSpec((None, Tp, D), lambda b, m: (b, 0, 0)),
                  pl.BlockSpec((None, Tp, LANES), lambda b, m: (b, 0, 0)),
                  pl.BlockSpec((None, tm, D), lambda b, m: (b, m, 0)),
                  _const_spec(wf.shape), _const_spec(wn.shape)],
        out_specs=pl.BlockSpec((None, tm, D), lambda b, m: (b, m, 0)),
        out_shape=jax.ShapeDtypeStruct((B, Tp, D), F32),
        compiler_params=pltpu.CompilerParams(
            dimension_semantics=("parallel", "parallel"), vmem_limit_bytes=VMEM_LIMIT_BYTES),
        name="fnet_t",
    )(ct, st, y, ynyq, h, wf, wn)


def _dft_table(n, rows, cols):
    j = lax.broadcasted_iota(jnp.int32, (rows, cols), 0)
    k = lax.broadcasted_iota(jnp.int32, (rows, cols), 1)
    ang = ((j * k) % n).astype(F32) * (2.0 * math.pi / n)
    return jnp.cos(ang), jnp.sin(ang)


def _time_dft_matrices(tr, tp):
    nj1 = tp // CHUNK
    k1 = lax.broadcasted_iota(jnp.int32, (nj1, tp), 1)
    j1 = lax.broadcasted_iota(jnp.int32, (nj1, tp), 0)
    ang_a = ((CHUNK * j1 * k1) % tr).astype(F32) * (2.0 * math.pi / tr)
    ca, sa = jnp.cos(ang_a)[:, None, :], jnp.sin(ang_a)[:, None, :]
    cb, sb = _dft_table(tr, CHUNK, tp)
    cb, sb = cb[None], sb[None]
    j = lax.broadcasted_iota(jnp.int32, (tp, tp), 0)
    k = lax.broadcasted_iota(jnp.int32, (tp, tp), 1)
    ok = jnp.logical_and(j < tr, k < tr)
    scale = 1.0 / math.sqrt(tr)
    c = jnp.where(ok, (ca * cb - sa * sb).reshape(tp, tp) * scale, 0.0)
    s = jnp.where(ok, (sa * cb + ca * sb).reshape(tp, tp) * scale, 0.0)
    return c.astype(BF16), s.astype(BF16)


def _channel_dft_tables():
    n, half = FNET_GROUP_DIM, FNET_GROUP_DIM // 2
    c = np.arange(n)[:, None]
    m = np.arange(half)[None, :]
    cos = np.cos(2 * np.pi * ((c * m) % n) / n)
    sin = np.sin(2 * np.pi * ((c * m) % n) / n)
    alt = np.where(np.arange(n) % 2 == 0, 1.0, -1.0)
    sin[:, 0] = alt
    cs2 = np.zeros((2 * n, 2 * n))
    for q in range(2):
        cs2[q * n:(q + 1) * n, q * half:(q + 1) * half] = cos
        cs2[q * n:(q + 1) * n, n + q * half:n + (q + 1) * half] = sin
    nyq = np.zeros((D_MODEL, LANES))
    for g in range(FNET_GROUPS):
        nyq[g * n:(g + 1) * n, g] = alt
    scale = 1.0 / math.sqrt(n)
    return jnp.asarray(cs2 * scale, BF16), jnp.asarray(nyq * scale, BF16)


def _fold_fnet_weight(w_f):
    n, half = FNET_GROUP_DIM, FNET_GROUP_DIM // 2
    w = w_f.reshape(FNET_GROUPS, n, D_MODEL)
    mirror = jnp.pad(jnp.flip(w[:, half + 1:], axis=1), ((0, 0), (1, 0), (0, 0)))
    w_cos = w[:, :half] + mirror
    w_sin = (mirror - w[:, :half]).at[:, 0].set(0.0)
    wf = jnp.concatenate([w_cos.reshape(HALF_D, D_MODEL), w_sin.reshape(HALF_D, D_MODEL)], axis=0)
    wn = jnp.pad(w[:, half], ((0, LANES - FNET_GROUPS), (0, 0)))
    return wf.astype(BF16), wn.astype(BF16)


def _largest_divisor(n, unit, cap):
    best = unit
    for m in range(unit, cap + 1, unit):
        if n % m == 0:
            best = m
    return best


BF16_ROWS = 16
LANES = 128
ROW_BLOCK_CAP = 448
DFT_BLOCK_BYTES_CAP = 7 * 1024 * 1024 // 2
FFN_ROW_BLOCK_CAP = 896


def _tiles(tp):
    tb = _largest_divisor(tp, BF16_ROWS, ROW_BLOCK_CAP)
    tm = _largest_divisor(tp, BF16_ROWS, max(BF16_ROWS, DFT_BLOCK_BYTES_CAP // (2 * tp)))
    tf = _largest_divisor(tp, NEXT_ROWS, FFN_ROW_BLOCK_CAP)
    return tb, tm, tf


def _prepare(meta_tokens, norm_mix, norm_ffn, norm_final, rwkv_mu, rwkv_w_rkv, rwkv_w0, rwkv_w1, rwkv_w2,
             rwkv_a0, rwkv_a1, rwkv_a2, rwkv_g1, rwkv_g2, rwkv_k_k, rwkv_k_a, rwkv_r_k, rwkv_gn_w, rwkv_gn_b,
             rwkv_w_o, fnet_w_o, ffn_w_in, ffn_conv_w, ffn_conv_b, ffn_w_out):
    D = D_MODEL
    row = lambda x: x.reshape(1, -1).astype(F32)

    def lora_pair(w1, w2, w0):
        first = jnp.concatenate([w1[0], w1[1]], axis=1)
        z = jnp.zeros_like(w2[0])
        second = jnp.concatenate([jnp.concatenate([w2[0], z], axis=1),
                                  jnp.concatenate([z, w2[1]], axis=1)], axis=0)
        return first, second.astype(BF16), jnp.concatenate([w0[0], w0[1]]).reshape(1, 2 * D)

    w1, w2, w0 = lora_pair(rwkv_w1[0], rwkv_w2[0], rwkv_w0[0])
    a1, a2, a0 = lora_pair(rwkv_a1[0], rwkv_a2[0], rwkv_a0[0])
    rank = rwkv_g1.shape[-1]
    g1 = jnp.pad(rwkv_g1[0], ((0, 0), (0, GATE_LORA_PAD - rank)))
    g2 = jnp.pad(rwkv_g2[0], ((0, GATE_LORA_PAD - rank), (0, 0))).astype(BF16)
    lora1 = jnp.concatenate([w1, a1, g1], axis=1).astype(BF16)
    head_of_lane = np.arange(D) // HEAD_SIZE
    e = (head_of_lane[:, None] == np.arange(128)[None, :]).astype(np.float32)
    cs2, nyq = _channel_dft_tables()
    wf, wn = _fold_fnet_weight(fnet_w_o[0])
    tri_f = np.tril(np.ones((CHUNK, CHUNK), np.float32))
    return dict(
        meta=meta_tokens,
        norm_mix0=row(norm_mix[0]), norm_mix1=row(norm_mix[1]),
        norm_ffn0=row(norm_ffn[0]), norm_ffn1=row(norm_ffn[1]), norm_final=row(norm_final),
        mu=jnp.pad(rwkv_mu[0], ((0, 2), (0, 0))),
        w_rkv=rwkv_w_rkv[0].astype(BF16), lora1=lora1, w2=w2, w0=w0, a2=a2, a0=a0, g2=g2,
        k_k=row(rwkv_k_k[0]), k_a=row(rwkv_k_a[0]), r_k=row(rwkv_r_k[0]),
        gn_w=row(rwkv_gn_w[0]), gn_b=row(rwkv_gn_b[0]),
        w_o=rwkv_w_o[0].astype(BF16), wf=wf, wn=wn, cs2=cs2, nyq=nyq,
        ffn_w_in=ffn_w_in.astype(BF16), ffn_conv_w=ffn_conv_w, ffn_conv_b=ffn_conv_b.reshape(-1, 1, D_FF),
        ffn_w_out=ffn_w_out.astype(BF16),
        E=jnp.asarray(e, BF16), ET=jnp.asarray(np.concatenate([e.T, e.T]), BF16),
        tri=jnp.asarray(np.stack([np.tile(tri_f, (1, 2)), np.tile(tri_f.T, (1, 2))]), BF16),
    )


def _trunk(x, p):
    B, T, D = x.shape
    tr = T + N_META
    tp = -(-tr // CHUNK) * CHUNK
    tb, tm, tf = _tiles(tp)
    meta = jnp.broadcast_to(p['meta'].astype(x.dtype)[None], (B, N_META, D))
    h = jnp.concatenate([meta, x, jnp.zeros((B, tp - tr, D), x.dtype)], axis=1)

    r, k, v, kk, g, a, lw, srk = _rwkv_proj(h, p, tb, tr)
    o_f, o_b = _wkv(r, k, v, kk, a, lw, p['k_a'], p['tri'])
    h = _rwkv_out(o_f, o_b, v, g, srk, h, p, tb)
    h, y, ynyq = _ffn(h, p['norm_ffn0'], p['ffn_w_in'][0], p['ffn_conv_w'][0], p['ffn_conv_b'][0],
                      p['ffn_w_out'][0], p['norm_mix1'], p['cs2'], p['nyq'], tf, tr, out_rows=tp, tail='fnet_c')
    ct, st = _time_dft_matrices(tr, tp)
    h = _fnet_t(ct, st, y, ynyq, h, p['wf'], p['wn'], tm)
    return _ffn(h, p['norm_ffn1'], p['ffn_w_in'][1], p['ffn_conv_w'][1], p['ffn_conv_b'][1], p['ffn_w_out'][1],
                p['norm_final'], p['cs2'], p['nyq'], tf, tr, out_rows=T, tail='final_norm')[0]


def kernel(x_prompt, x_sample, meta_tokens, norm_mix, norm_ffn, norm_final, rwkv_mu, rwkv_w_rkv, rwkv_w0, rwkv_w1, rwkv_w2, rwkv_a0, rwkv_a1, rwkv_a2, rwkv_g1, rwkv_g2, rwkv_k_k, rwkv_k_a, rwkv_r_k, rwkv_gn_w, rwkv_gn_b, rwkv_w_o, fnet_w_o, ffn_w_in, ffn_conv_w, ffn_conv_b, ffn_w_out):
    p = _prepare(meta_tokens, norm_mix, norm_ffn, norm_final, rwkv_mu, rwkv_w_rkv, rwkv_w0, rwkv_w1, rwkv_w2,
                 rwkv_a0, rwkv_a1, rwkv_a2, rwkv_g1, rwkv_g2, rwkv_k_k, rwkv_k_a, rwkv_r_k, rwkv_gn_w,
                 rwkv_gn_b, rwkv_w_o, fnet_w_o, ffn_w_in, ffn_conv_w, ffn_conv_b, ffn_w_out)
    return (_trunk(x_prompt, p), _trunk(x_sample, p))
```

```python
import functools
import math

import jax
import jax.numpy as jnp
import numpy as np
from jax import lax
from jax.experimental import pallas as pl
from jax.experimental.pallas import tpu as pltpu

D_MODEL = 1024
N_META = 16
HEAD_SIZE = 64
N_HEADS = D_MODEL // HEAD_SIZE
GATE_LORA_PAD = 256
LORA2 = 128
D_FF = 2816
FNET_GROUPS = 8
FNET_GROUP_DIM = D_MODEL // FNET_GROUPS
HALF_D = D_MODEL // 2
RMS_EPS = 1e-6
GN_EPS = 64e-5
LOG2_E = math.log2(math.e)

CHUNK = 64
PAIR = 2 * HEAD_SIZE
N_PAIRS = N_HEADS // 2
HALO = 8
FF_CHUNK = 256
META_SHIFT = N_META
NEXT_ROWS = 32
WKV_SUBCHUNK_CAP = 5
VMEM_LIMIT_BYTES = 56 * 1024 * 1024

BF16 = jnp.bfloat16
F32 = jnp.float32


def _dot(a, b):
    return jnp.dot(a, b, preferred_element_type=F32)


def _dot_nt(a, b):
    return lax.dot_general(a, b, (((1,), (1,)), ((), ())), preferred_element_type=F32)


def _dot_tn(a, b):
    return lax.dot_general(a, b, (((0,), (0,)), ((), ())), preferred_element_type=F32)


def _rmsnorm(x, gain):
    ms = jnp.mean(x * x, axis=-1, keepdims=True)
    return x * lax.rsqrt(ms + RMS_EPS) * gain


def _split_bf16(x):
    hi = x.astype(BF16)
    lo = (x - hi.astype(F32)).astype(BF16)
    return hi, lo


def _head_sum(x, e_ref):
    return _dot(x.astype(BF16), e_ref[...])


def _head_bcast(s, et2_ref):
    hi, lo = _split_bf16(s)
    return _dot(jnp.concatenate([hi, lo], axis=1), et2_ref[...])


def _normed_window(h_ref, hp_ref, hn_ref, gain, i, nblk, tb, tr):
    row0 = i * tb
    rows = row0 + lax.broadcasted_iota(jnp.int32, (tb, 1), 0)
    valid = rows < tr
    x = jnp.where(valid, _rmsnorm(h_ref[...], gain), 0.0)
    xp = _rmsnorm(hp_ref[...], gain)[HALO - 1:HALO, :]
    xp = jnp.where(i > 0, xp, 0.0)
    xn = _rmsnorm(hn_ref[...], gain)[0:1, :]
    xn = jnp.where(jnp.logical_and(i < nblk - 1, row0 + tb < tr), xn, 0.0)
    return x, xp, xn, valid


def _shift_rows(x, xp, xn, tb):
    ridx = lax.broadcasted_iota(jnp.int32, (tb, 1), 0)
    x_prev = jnp.where(ridx == 0, xp, pltpu.roll(x, 1, 0))
    x_next = jnp.where(ridx == tb - 1, xn, pltpu.roll(x, tb - 1, 0))
    return x_prev, x_next


def _proj_kernel(h_ref, hp_ref, hn_ref, gain_ref, mu_ref, wrkv_ref, lora1_ref, w2_ref, w0_ref,
                 a2_ref, a0_ref, g2_ref, kk_ref, rk_ref, ka_ref, e_ref, et_ref,
                 r_out, k_out, v_out, kk_out, g_out, a_out, lw_out, srk_out, *, tb, tr, nblk):
    i = pl.program_id(1)
    x, xp, xn, valid = _normed_window(h_ref, hp_ref, hn_ref, gain_ref[...], i, nblk, tb, tr)
    x_prev, x_next = _shift_rows(x, xp, xn, tb)
    xx = jnp.where(valid, 0.5 * (x_prev + x_next) - x, 0.0)
    mu = mu_ref[...]

    def mix(j):
        return (x + xx * mu[j:j + 1, :]).astype(BF16)

    k = _dot(mix(2), wrkv_ref[1])
    tw = _dot(mix(1), lora1_ref[:, :LORA2])
    ta = _dot(mix(4), lora1_ref[:, LORA2:2 * LORA2])
    tg = _dot(mix(5), lora1_ref[:, 2 * LORA2:])
    r = _dot(mix(0), wrkv_ref[0])
    kkr = k * kk_ref[...]
    ssq = _head_sum(kkr * kkr, e_ref)
    v = _dot(mix(3), wrkv_ref[2])
    w_lin = w0_ref[...] + _dot(jnp.tanh(tw).astype(BF16), w2_ref[...])
    a_lin = a0_ref[...] + _dot(ta.astype(BF16), a2_ref[...])
    gate = _dot(jax.nn.sigmoid(tg).astype(BF16), g2_ref[...])
    inv = _head_bcast(1.0 / jnp.maximum(jnp.sqrt(ssq), 1e-12), et_ref)

    k_out[...] = k.astype(BF16)
    r_out[...] = r.astype(BF16)
    v_out[...] = v.astype(BF16)
    lw_out[...] = (-math.exp(-0.5) * LOG2_E) * jax.nn.sigmoid(w_lin)
    a_sig = jax.nn.sigmoid(a_lin)
    a_out[...] = a_sig.astype(BF16)
    g_out[...] = gate.astype(BF16)
    kk_out[...] = (kkr * inv).astype(BF16)
    a_sum = a_sig[:, :D_MODEL] + a_sig[:, D_MODEL:]
    srk_out[...] = _head_sum(r * rk_ref[...] * k * (2.0 + (a_sum - 2.0) * ka_ref[...]), e_ref)


def _row_block_specs(tb, nhalo_blocks):
    per = tb // HALO
    main = pl.BlockSpec((None, tb, D_MODEL), lambda b, i, *_: (b, i, 0))
    prev = pl.BlockSpec((None, HALO, D_MODEL), lambda b, i, *_: (b, jnp.maximum(i * per - 1, 0), 0))
    nxt = pl.BlockSpec((None, HALO, D_MODEL),
                       lambda b, i, *_: (b, jnp.minimum((i + 1) * per, nhalo_blocks - 1), 0))
    return main, prev, nxt


def _const_spec(shape):
    nd = len(shape)
    return pl.BlockSpec(shape, lambda *_: (0,) * nd)


def _rwkv_proj(h, p, tb, tr):
    B, Tp, D = h.shape
    nblk = Tp // tb
    main, prev, nxt = _row_block_specs(tb, Tp // HALO)
    row = lambda w: pl.BlockSpec((None, tb, w), lambda b, i: (b, i, 0))
    consts = [p['norm_mix0'], p['mu'], p['w_rkv'], p['lora1'], p['w2'], p['w0'], p['a2'], p['a0'],
              p['g2'], p['k_k'], p['r_k'], p['k_a'], p['E'], p['ET']]
    out_shape = [jax.ShapeDtypeStruct((B, Tp, D), BF16)] * 5 + [
        jax.ShapeDtypeStruct((B, Tp, 2 * D), BF16), jax.ShapeDtypeStruct((B, Tp, 2 * D), F32),
        jax.ShapeDtypeStruct((B, Tp, LANES), F32)]
    return pl.pallas_call(
        functools.partial(_proj_kernel, tb=tb, tr=tr, nblk=nblk),
        grid=(B, nblk),
        in_specs=[main, prev, nxt] + [_const_spec(c.shape) for c in consts],
        out_specs=[row(D)] * 5 + [row(2 * D), row(2 * D), row(LANES)],
        out_shape=out_shape,
        compiler_params=pltpu.CompilerParams(
            dimension_semantics=("parallel", "arbitrary"), vmem_limit_bytes=VMEM_LIMIT_BYTES),
        name="rwkv_proj",
    )(h, h, h, *consts)


def _wkv_kernel(rf_ref, kf_ref, vf_ref, kkf_ref, af_ref, lwf_ref,
                rb_ref, kb_ref, vb_ref, kkb_ref, ab_ref, lwb_ref,
                ka_ref, tri_ref, of_ref, ob_ref, state_ref, *, nsub):
    c = pl.program_id(1)

    @pl.when(c == 0)
    def _():
        state_ref[...] = jnp.zeros_like(state_ref)

    for sub in range(nsub):
        chains = _wkv_prep(rf_ref, kf_ref, vf_ref, kkf_ref, af_ref, lwf_ref, rb_ref, kb_ref, vb_ref, kkb_ref,
                           ab_ref, lwb_ref, ka_ref, tri_ref, of_ref, ob_ref,
                           pl.ds(sub * CHUNK, CHUNK), pl.ds((nsub - 1 - sub) * CHUNK, CHUNK))
        for ch in chains:
            _wkv_scores(ch)
        _wkv_solve(chains, state_ref)


def _wkv_scores(ch):
    ch['a_all'] = _dot_nt(ch['ar'], ch['bk'])


def _wkv_prep(rf_ref, kf_ref, vf_ref, kkf_ref, af_ref, lwf_ref, rb_ref, kb_ref, vb_ref, kkb_ref, ab_ref, lwb_ref,
              ka_ref, tri_ref, of_ref, ob_ref, rows_f, rows_b):
    L = CHUNK
    t_idx = lax.broadcasted_iota(jnp.int32, (L, PAIR), 0)
    lane = lax.broadcasted_iota(jnp.int32, (L, PAIR), 1)
    s_idx = lane % HEAD_SIZE
    head0 = lax.broadcasted_iota(jnp.int32, (1, PAIR), 1) < HEAD_SIZE
    ka = ka_ref[...]
    one_minus_ka = 1.0 - ka

    dirs = (
        (0, rows_f, rf_ref, kf_ref, vf_ref, kkf_ref, af_ref, lwf_ref, of_ref, s_idx < t_idx, s_idx <= t_idx, L - 1),
        (1, rows_b, rb_ref, kb_ref, vb_ref, kkb_ref, ab_ref, lwb_ref, ob_ref, s_idx > t_idx, s_idx >= t_idx, 0),
    )
    chains = []
    for d, rows, r_ref, k_ref, v_ref, kk_ref, a_ref, lw_ref, o_ref, strict, incl, last in dirs:
        lw = lw_ref[rows, :]
        hi, lo = _split_bf16(lw)
        cum = _dot(tri_ref[d], jnp.concatenate([hi, lo], axis=0))
        cum_last = cum[last:last + 1, :]
        g_incl = jnp.exp2(cum)
        g_excl = jnp.exp2(cum - lw)
        g_inv = jnp.exp2(-cum)
        g_tail = jnp.exp2(cum_last - cum)
        g_last = jnp.exp2(cum_last)

        kk = kk_ref[rows, :].astype(F32)
        a = a_ref[rows, :].astype(F32)
        kdir = k_ref[rows, :].astype(F32) * (a * ka + one_minus_ka)
        b = kk * a
        a_t = (-kk * g_excl).astype(BF16)
        r_t = (r_ref[rows, :].astype(F32) * g_incl).astype(BF16)
        b_t = b * g_inv
        k_t = kdir * g_inv
        b_h = (b * g_tail).astype(BF16)
        k_h = (kdir * g_tail).astype(BF16)
        v = v_ref[rows, :]

        for p in range(N_PAIRS):
            sl = slice(p * PAIR, (p + 1) * PAIR)
            bp, kp, vp = b_t[:, sl], k_t[:, sl], v[:, sl]
            chains.append(dict(
                d=d, p=p, sl=sl, rows=rows, o_ref=o_ref, strict=strict, incl=incl, vp=vp,
                ar=jnp.concatenate([a_t[:, sl], r_t[:, sl]], axis=0),
                bk=jnp.concatenate([jnp.where(head0, bp, 0.0), jnp.where(head0, 0.0, bp),
                                    jnp.where(head0, kp, 0.0), jnp.where(head0, 0.0, kp)],
                                   axis=0).astype(BF16),
                v_bd=jnp.concatenate([jnp.where(head0, vp, 0), jnp.where(head0, 0, vp)], axis=0),
                bk_h=jnp.concatenate([b_h[:, sl], k_h[:, sl]], axis=0),
                g_last=g_last[:, sl]))
    return chains


def _wkv_solve(chains, state_ref):
    L = CHUNK
    head0 = lax.broadcasted_iota(jnp.int32, (1, PAIR), 1) < HEAD_SIZE
    steps = int(math.log2(L))
    for ch in chains:
        ch['s0'] = state_ref[ch['d'], ch['p']]
        s0b = ch['s0'].astype(BF16)
        s0b = jnp.concatenate([jnp.where(head0, s0b[:HEAD_SIZE], 0), jnp.where(head0, 0, s0b[HEAD_SIZE:])], axis=0)
        ch['ah'] = _dot_nt(ch['ar'], s0b)
    for ch in chains:
        a_all, strict, incl = ch['a_all'], ch['strict'], ch['incl']
        ch['pk'] = jnp.where(strict, a_all[:L, :PAIR], 0.0)
        a_ak = jnp.where(strict, a_all[:L, PAIR:], 0.0).astype(BF16)
        ch['a_rbk'] = jnp.concatenate([jnp.where(incl, a_all[L:, :PAIR], 0.0),
                                       jnp.where(incl, a_all[L:, PAIR:], 0.0)], axis=1).astype(BF16)
        ch['w'] = ch['ah'][:L] + _dot(a_ak, ch['v_bd'])
    for step in range(steps):
        for ch in chains:
            pkb = ch['pk'].astype(BF16)
            wb = ch['w'].astype(BF16)
            if step < steps - 1:
                rhs = jnp.concatenate(
                    [jnp.concatenate([jnp.where(head0, pkb, 0), jnp.where(head0, wb, 0)], axis=1),
                     jnp.concatenate([jnp.where(head0, 0, pkb), jnp.where(head0, 0, wb)], axis=1)], axis=0)
                out = _dot(pkb, rhs)
                ch['pk'] = out[:, :PAIR]
                ch['w'] = ch['w'] + out[:, PAIR:]
            else:
                rhs = jnp.concatenate([jnp.where(head0, wb, 0), jnp.where(head0, 0, wb)], axis=0)
                ch['w'] = ch['w'] + _dot(pkb, rhs)
    for ch in chains:
        ub = ch['w'].astype(BF16)
        uv_bd = jnp.concatenate([jnp.where(head0, ub, 0), jnp.where(head0, 0, ub), ch['v_bd']], axis=0)
        ch['o_ref'][ch['rows'], ch['sl']] = (ch['ah'][L:] + _dot(ch['a_rbk'], uv_bd)).astype(ch['o_ref'].dtype)
        upd = _dot_tn(jnp.concatenate([ub, ch['vp']], axis=0), ch['bk_h'])
        state_ref[ch['d'], ch['p']] = ch['s0'] * ch['g_last'] + upd


def _wkv(r, k, v, kk, a, lw, k_a, tri):
    B, Tp, D = r.shape
    nsub = _largest_divisor(Tp // CHUNK, 1, WKV_SUBCHUNK_CAP)
    rows = nsub * CHUNK
    n = Tp // rows
    fwd = lambda col: pl.BlockSpec((None, rows, D), lambda b, c: (b, c, col))
    bwd = lambda col: pl.BlockSpec((None, rows, D), lambda b, c: (b, n - 1 - c, col))
    in_specs = ([fwd(0)] * 6 + [bwd(0)] * 4 + [bwd(1), bwd(1)]
                + [_const_spec(k_a.shape), _const_spec(tri.shape)])
    return pl.pallas_call(
        functools.partial(_wkv_kernel, nsub=nsub),
        grid=(B, n),
        in_specs=in_specs,
        out_specs=[fwd(0), bwd(0)],
        out_shape=[jax.ShapeDtypeStruct((B, Tp, D), BF16)] * 2,
        scratch_shapes=[pltpu.VMEM((2, N_PAIRS, PAIR, PAIR), F32)],
        compiler_params=pltpu.CompilerParams(
            dimension_semantics=("parallel", "arbitrary"), vmem_limit_bytes=VMEM_LIMIT_BYTES),
        name="wkv",
    )(r, k, v, kk, a, lw, r, k, v, kk, a, lw, k_a, tri)


def _rwkv_out_kernel(of_ref, ob_ref, v_ref, g_ref, srk_ref, h_ref,
                     gnw_ref, gnb_ref, wo_ref, e_ref, et_ref, out_ref, *, tb):
    inv_n = 1.0 / HEAD_SIZE
    half = tb // 2
    parts = [dict(rows=slice(q * half, (q + 1) * half)) for q in range(2)]
    for pt in parts:
        rows = pt['rows']
        pt['o'] = of_ref[rows, :].astype(F32) + ob_ref[rows, :].astype(F32)
        pt['s_o'] = _head_sum(pt['o'], e_ref)
    for pt in parts:
        pt['dlt'] = pt['o'] - _head_bcast(pt['s_o'] * inv_n, et_ref)
        pt['bonus'] = _head_bcast(srk_ref[pt['rows'], :], et_ref)
    for pt in parts:
        pt['s_v'] = _head_sum(pt['dlt'] * pt['dlt'], e_ref)
    for pt in parts:
        rows = pt['rows']
        var = _head_bcast(pt['s_v'] * inv_n, et_ref)
        on = pt['dlt'] * lax.rsqrt(var + GN_EPS) * gnw_ref[...] + gnb_ref[...]
        on = on + pt['bonus'] * v_ref[rows, :].astype(F32)
        y = (on * g_ref[rows, :].astype(F32)).astype(BF16)
        out_ref[rows, :] = h_ref[rows, :] + _dot(y, wo_ref[...])


def _rwkv_out(o_f, o_b, v, g, srk, h, p, tb):
    B, Tp, D = h.shape
    row = lambda w=D: pl.BlockSpec((None, tb, w), lambda b, i: (b, i, 0))
    consts = [p['gn_w'], p['gn_b'], p['w_o'], p['E'], p['ET']]
    return pl.pallas_call(
        functools.partial(_rwkv_out_kernel, tb=tb),
        grid=(B, Tp // tb),
        in_specs=[row()] * 4 + [row(LANES), row()] + [_const_spec(c.shape) for c in consts],
        out_specs=row(),
        out_shape=jax.ShapeDtypeStruct((B, Tp, D), F32),
        compiler_params=pltpu.CompilerParams(
            dimension_semantics=("parallel", "parallel"), vmem_limit_bytes=VMEM_LIMIT_BYTES),
        name="rwkv_out",
    )(o_f, o_b, v, g, srk, h, *consts)


def _channel_dft(x, cs2_ref, nyq_ref, y_ref, ynyq_ref):
    cs2 = cs2_ref[...]
    for q in range(FNET_GROUPS // 2):
        y = _dot(x[:, q * 2 * FNET_GROUP_DIM:(q + 1) * 2 * FNET_GROUP_DIM], cs2).astype(BF16)
        y_ref[:, q * FNET_GROUP_DIM:(q + 1) * FNET_GROUP_DIM] = y[:, :FNET_GROUP_DIM]
        y_ref[:, HALF_D + q * FNET_GROUP_DIM:HALF_D + (q + 1) * FNET_GROUP_DIM] = y[:, FNET_GROUP_DIM:]
    ynyq_ref[...] = _dot(x, nyq_ref[...]).astype(BF16)


def _ffn_kernel(h_ref, hp_ref, hn_ref, gain_ref, win_ref, cw_ref, cb_ref, wout_ref, gtail_ref, cs2_ref, nyq_ref,
                out_ref, *y_refs, tb, tr, shift, tail):
    i = pl.program_id(1)
    n_win = tb + 2 * HALO
    if shift == 0:
        hw = jnp.concatenate([hp_ref[...], h_ref[...], hn_ref[0:HALO, :]], axis=0)
        h_res = h_ref[...]
    else:
        hw = jnp.concatenate([h_ref[shift - HALO:, :], hn_ref[0:shift + HALO, :]], axis=0)
        h_res = jnp.concatenate([h_ref[shift:, :], hn_ref[0:shift, :]], axis=0)
    rows = i * tb + (shift - HALO) + lax.broadcasted_iota(jnp.int32, (n_win, 1), 0)
    valid = jnp.logical_and(rows >= 0, rows < tr)
    xw = jnp.where(valid, _rmsnorm(hw, gain_ref[...]), 0.0).astype(BF16)
    x_mid = xw[HALO:HALO + tb]

    def in_dots(c0, width):
        u_act = _dot(xw, win_ref[:, c0:c0 + width])
        u_lin = _dot(x_mid, win_ref[:, D_FF + c0:D_FF + c0 + width])
        return u_act, u_lin

    chunks = [(c0, min(FF_CHUNK, D_FF - c0)) for c0 in range(0, D_FF, FF_CHUNK)]
    acc = None
    nxt = in_dots(*chunks[0])
    for ci, (c0, width) in enumerate(chunks):
        u_act, u_lin = nxt
        if ci + 1 < len(chunks):
            nxt = in_dots(*chunks[ci + 1])
        cw = cw_ref[:, c0:c0 + width]
        c = (pltpu.roll(u_act, 1, 0)[HALO:HALO + tb] * cw[0:1, :]
             + u_act[HALO:HALO + tb] * cw[1:2, :]
             + pltpu.roll(u_act, n_win - 1, 0)[HALO:HALO + tb] * cw[2:3, :]
             + cb_ref[:, c0:c0 + width])
        y = (c * jax.nn.sigmoid(c) * u_lin).astype(BF16)
        part = _dot(y, wout_ref[c0:c0 + width, :])
        acc = part if acc is None else acc + part
    res = h_res + acc
    if tail == 'final_norm':
        res = _rmsnorm(res, gtail_ref[...])
    out_ref[...] = res
    if tail == 'fnet_c':
        out_rows = i * tb + lax.broadcasted_iota(jnp.int32, (tb, 1), 0)
        xn = jnp.where(out_rows < tr, _rmsnorm(res, gtail_ref[...]), 0.0).astype(BF16)
        _channel_dft(xn, cs2_ref, nyq_ref, *y_refs)


def _ffn(h, gain, w_in, conv_w, conv_b, w_out, g_tail, cs2, nyq, tb, tr, out_rows, tail):
    B, Tp, D = h.shape
    shift = 0 if out_rows == Tp else META_SHIFT
    assert tail in ('final_norm', 'fnet_c') and (tail != 'fnet_c' or shift == 0)
    out_specs = [pl.BlockSpec((None, tb, D), lambda b, i: (b, i, 0))]
    out_shape = [jax.ShapeDtypeStruct((B, out_rows, D), F32)]
    if tail == 'fnet_c':
        out_specs += [pl.BlockSpec((None, tb, D), lambda b, i: (b, i, 0)),
                      pl.BlockSpec((None, tb, LANES), lambda b, i: (b, i, 0))]
        out_shape += [jax.ShapeDtypeStruct((B, Tp, D), BF16), jax.ShapeDtypeStruct((B, Tp, LANES), BF16)]
    nblk = pl.cdiv(out_rows, tb)
    per_prev, per_next, n_next = tb // HALO, tb // NEXT_ROWS, Tp // NEXT_ROWS
    resident = lambda shape: pl.BlockSpec(shape, lambda *_: (0,) * len(shape), pipeline_mode=pl.Buffered(1))
    in_specs = [
        pl.BlockSpec((None, tb, D), lambda b, i: (b, i, 0)),
        pl.BlockSpec((None, HALO, D), lambda b, i: (b, jnp.maximum(i * per_prev - 1, 0), 0)),
        pl.BlockSpec((None, NEXT_ROWS, D), lambda b, i: (b, jnp.minimum((i + 1) * per_next, n_next - 1), 0)),
        _const_spec(gain.shape), resident(w_in.shape), _const_spec(conv_w.shape), _const_spec(conv_b.shape),
        resident(w_out.shape), _const_spec(g_tail.shape), _const_spec(cs2.shape), _const_spec(nyq.shape),
    ]
    return pl.pallas_call(
        functools.partial(_ffn_kernel, tb=tb, tr=tr, shift=shift, tail=tail),
        grid=(B, nblk),
        in_specs=in_specs,
        out_specs=out_specs,
        out_shape=out_shape,
        compiler_params=pltpu.CompilerParams(
            dimension_semantics=("parallel", "parallel"), vmem_limit_bytes=VMEM_LIMIT_BYTES),
        name="ffn_" + tail,
    )(h, h, h, gain, w_in, conv_w, conv_b, w_out, g_tail, cs2, nyq)


def _fnet_t_kernel(ct_ref, st_ref, y_ref, ynyq_ref, h_ref, wf_ref, wn_ref, out_ref):
    ct = ct_ref[...]
    zc = _dot(ct, y_ref[:, :HALF_D])
    zs = _dot(st_ref[...], y_ref[:, HALF_D:])
    zn = _dot(ct, ynyq_ref[...])
    z = jnp.concatenate([zc, zs], axis=1).astype(BF16)
    out_ref[...] = h_ref[...] + _dot(z, wf_ref[...]) + _dot(zn.astype(BF16), wn_ref[...])


def _fnet_t(ct, st, y, ynyq, h, wf, wn, tm):
    B, Tp, D = h.shape
    return pl.pallas_call(
        _fnet_t_kernel,
        grid=(B, Tp // tm),
        in_specs=[pl.BlockSpec((tm, Tp), lambda b, m: (m, 0)),
                  pl.BlockSpec((tm, Tp), lambda b, m: (m, 0)),
                  pl.BlockSpec((None, Tp, D), lambda b, m: (b, 0, 0)),
                  pl.BlockSpec((None, Tp, LANES), lambda b, m: (b, 0, 0)),
                  pl.BlockSpec((None, tm, D), lambda b, m: (b, m, 0)),
                  _const_spec(wf.shape), _const_spec(wn.shape)],
        out_specs=pl.BlockSpec((None, tm, D), lambda b, m: (b, m, 0)),
        out_shape=jax.ShapeDtypeStruct((B, Tp, D), F32),
        compiler_params=pltpu.CompilerParams(
            dimension_semantics=("parallel", "parallel"), vmem_limit_bytes=VMEM_LIMIT_BYTES),
        name="fnet_t",
    )(ct, st, y, ynyq, h, wf, wn)


def _dft_table(n, rows, cols):
    j = lax.broadcasted_iota(jnp.int32, (rows, cols), 0)
    k = lax.broadcasted_iota(jnp.int32, (rows, cols), 1)
    ang = ((j * k) % n).astype(F32) * (2.0 * math.pi / n)
    return jnp.cos(ang), jnp.sin(ang)


def _time_dft_matrices(tr, tp):
    nj1 = tp // CHUNK
    k1 = lax.broadcasted_iota(jnp.int32, (nj1, tp), 1)
    j1 = lax.broadcasted_iota(jnp.int32, (nj1, tp), 0)
    ang_a = ((CHUNK * j1 * k1) % tr).astype(F32) * (2.0 * math.pi / tr)
    ca, sa = jnp.cos(ang_a)[:, None, :], jnp.sin(ang_a)[:, None, :]
    cb, sb = _dft_table(tr, CHUNK, tp)
    cb, sb = cb[None], sb[None]
    j = lax.broadcasted_iota(jnp.int32, (tp, tp), 0)
    k = lax.broadcasted_iota(jnp.int32, (tp, tp), 1)
    ok = jnp.logical_and(j < tr, k < tr)
    scale = 1.0 / math.sqrt(tr)
    c = jnp.where(ok, (ca * cb - sa * sb).reshape(tp, tp) * scale, 0.0)
    s = jnp.where(ok, (sa * cb + ca * sb).reshape(tp, tp) * scale, 0.0)
    return c.astype(BF16), s.astype(BF16)


def _channel_dft_tables():
    n, half = FNET_GROUP_DIM, FNET_GROUP_DIM // 2
    c = np.arange(n)[:, None]
    m = np.arange(half)[None, :]
    cos = np.cos(2 * np.pi * ((c * m) % n) / n)
    sin = np.sin(2 * np.pi * ((c * m) % n) / n)
    alt = np.where(np.arange(n) % 2 == 0, 1.0, -1.0)
    sin[:, 0] = alt
    cs2 = np.zeros((2 * n, 2 * n))
    for q in range(2):
        cs2[q * n:(q + 1) * n, q * half:(q + 1) * half] = cos
        cs2[q * n:(q + 1) * n, n + q * half:n + (q + 1) * half] = sin
    nyq = np.zeros((D_MODEL, LANES))
    for g in range(FNET_GROUPS):
        nyq[g * n:(g + 1) * n, g] = alt
    scale = 1.0 / math.sqrt(n)
    return jnp.asarray(cs2 * scale, BF16), jnp.asarray(nyq * scale, BF16)


def _fold_fnet_weight(w_f):
    n, half = FNET_GROUP_DIM, FNET_GROUP_DIM // 2
    w = w_f.reshape(FNET_GROUPS, n, D_MODEL)
    mirror = jnp.pad(jnp.flip(w[:, half + 1:], axis=1), ((0, 0), (1, 0), (0, 0)))
    w_cos = w[:, :half] + mirror
    w_sin = (mirror - w[:, :half]).at[:, 0].set(0.0)
    wf = jnp.concatenate([w_cos.reshape(HALF_D, D_MODEL), w_sin.reshape(HALF_D, D_MODEL)], axis=0)
    wn = jnp.pad(w[:, half], ((0, LANES - FNET_GROUPS), (0, 0)))
    return wf.astype(BF16), wn.astype(BF16)


def _largest_divisor(n, unit, cap):
    best = unit
    for m in range(unit, cap + 1, unit):
        if n % m == 0:
            best = m
    return best


BF16_ROWS = 16
LANES = 128
ROW_BLOCK_CAP = 448
DFT_BLOCK_BYTES_CAP = 7 * 1024 * 1024 // 2
FFN_ROW_BLOCK_CAP = 896


def _tiles(tp):
    tb = _largest_divisor(tp, BF16_ROWS, ROW_BLOCK_CAP)
    tm = _largest_divisor(tp, BF16_ROWS, max(BF16_ROWS, DFT_BLOCK_BYTES_CAP // (2 * tp)))
    tf = _largest_divisor(tp, NEXT_ROWS, FFN_ROW_BLOCK_CAP)
    return tb, tm, tf


def _prepare(meta_tokens, norm_mix, norm_ffn, norm_final, rwkv_mu, rwkv_w_rkv, rwkv_w0, rwkv_w1, rwkv_w2,
             rwkv_a0, rwkv_a1, rwkv_a2, rwkv_g1, rwkv_g2, rwkv_k_k, rwkv_k_a, rwkv_r_k, rwkv_gn_w, rwkv_gn_b,
             rwkv_w_o, fnet_w_o, ffn_w_in, ffn_conv_w, ffn_conv_b, ffn_w_out):
    D = D_MODEL
    row = lambda x: x.reshape(1, -1).astype(F32)

    def lora_pair(w1, w2, w0):
        first = jnp.concatenate([w1[0], w1[1]], axis=1)
        z = jnp.zeros_like(w2[0])
        second = jnp.concatenate([jnp.concatenate([w2[0], z], axis=1),
                                  jnp.concatenate([z, w2[1]], axis=1)], axis=0)
        return first, second.astype(BF16), jnp.concatenate([w0[0], w0[1]]).reshape(1, 2 * D)

    w1, w2, w0 = lora_pair(rwkv_w1[0], rwkv_w2[0], rwkv_w0[0])
    a1, a2, a0 = lora_pair(rwkv_a1[0], rwkv_a2[0], rwkv_a0[0])
    rank = rwkv_g1.shape[-1]
    g1 = jnp.pad(rwkv_g1[0], ((0, 0), (0, GATE_LORA_PAD - rank)))
    g2 = jnp.pad(rwkv_g2[0], ((0, GATE_LORA_PAD - rank), (0, 0))).astype(BF16)
    lora1 = jnp.concatenate([w1, a1, g1], axis=1).astype(BF16)
    head_of_lane = np.arange(D) // HEAD_SIZE
    e = (head_of_lane[:, None] == np.arange(128)[None, :]).astype(np.float32)
    cs2, nyq = _channel_dft_tables()
    wf, wn = _fold_fnet_weight(fnet_w_o[0])
    tri_f = np.tril(np.ones((CHUNK, CHUNK), np.float32))
    return dict(
        meta=meta_tokens,
        norm_mix0=row(norm_mix[0]), norm_mix1=row(norm_mix[1]),
        norm_ffn0=row(norm_ffn[0]), norm_ffn1=row(norm_ffn[1]), norm_final=row(norm_final),
        mu=jnp.pad(rwkv_mu[0], ((0, 2), (0, 0))),
        w_rkv=rwkv_w_rkv[0].astype(BF16), lora1=lora1, w2=w2, w0=w0, a2=a2, a0=a0, g2=g2,
        k_k=row(rwkv_k_k[0]), k_a=row(rwkv_k_a[0]), r_k=row(rwkv_r_k[0]),
        gn_w=row(rwkv_gn_w[0]), gn_b=row(rwkv_gn_b[0]),
        w_o=rwkv_w_o[0].astype(BF16), wf=wf, wn=wn, cs2=cs2, nyq=nyq,
        ffn_w_in=ffn_w_in.astype(BF16), ffn_conv_w=ffn_conv_w, ffn_conv_b=ffn_conv_b.reshape(-1, 1, D_FF),
        ffn_w_out=ffn_w_out.astype(BF16),
        E=jnp.asarray(e, BF16), ET=jnp.asarray(np.concatenate([e.T, e.T]), BF16),
        tri=jnp.asarray(np.stack([np.tile(tri_f, (1, 2)), np.tile(tri_f.T, (1, 2))]), BF16),
    )


def _trunk(x, p):
    B, T, D = x.shape
    tr = T + N_META
    tp = -(-tr // CHUNK) * CHUNK
    tb, tm, tf = _tiles(tp)
    meta = jnp.broadcast_to(p['meta'].astype(x.dtype)[None], (B, N_META, D))
    h = jnp.concatenate([meta, x, jnp.zeros((B, tp - tr, D), x.dtype)], axis=1)

    r, k, v, kk, g, a, lw, srk = _rwkv_proj(h, p, tb, tr)
    o_f, o_b = _wkv(r, k, v, kk, a, lw, p['k_a'], p['tri'])
    h = _rwkv_out(o_f, o_b, v, g, srk, h, p, tf)
    h, y, ynyq = _ffn(h, p['norm_ffn0'], p['ffn_w_in'][0], p['ffn_conv_w'][0], p['ffn_conv_b'][0],
                      p['ffn_w_out'][0], p['norm_mix1'], p['cs2'], p['nyq'], tf, tr, out_rows=tp, tail='fnet_c')
    ct, st = _time_dft_matrices(tr, tp)
    h = _fnet_t(ct, st, y, ynyq, h, p['wf'], p['wn'], tm)
    return _ffn(h, p['norm_ffn1'], p['ffn_w_in'][1], p['ffn_conv_w'][1], p['ffn_conv_b'][1], p['ffn_w_out'][1],
                p['norm_final'], p['cs2'], p['nyq'], tf, tr, out_rows=T, tail='final_norm')[0]


def kernel(x_prompt, x_sample, meta_tokens, norm_mix, norm_ffn, norm_final, rwkv_mu, rwkv_w_rkv, rwkv_w0, rwkv_w1, rwkv_w2, rwkv_a0, rwkv_a1, rwkv_a2, rwkv_g1, rwkv_g2, rwkv_k_k, rwkv_k_a, rwkv_r_k, rwkv_gn_w, rwkv_gn_b, rwkv_w_o, fnet_w_o, ffn_w_in, ffn_conv_w, ffn_conv_b, ffn_w_out):
    p = _prepare(meta_tokens, norm_mix, norm_ffn, norm_final, rwkv_mu, rwkv_w_rkv, rwkv_w0, rwkv_w1, rwkv_w2,
                 rwkv_a0, rwkv_a1, rwkv_a2, rwkv_g1, rwkv_g2, rwkv_k_k, rwkv_k_a, rwkv_r_k, rwkv_gn_w,
                 rwkv_gn_b, rwkv_w_o, fnet_w_o, ffn_w_in, ffn_conv_w, ffn_conv_b, ffn_w_out)
    return (_trunk(x_prompt, p), _trunk(x_sample, p))
```
